```python
import jax, jax.numpy as jnp
from jax import lax
import numpy as np

D_MODEL = 2048
BATCH = 8
SEQ = 2048
DEPTH = 1

GLA_HEADS = 8
GLA_DK = 128
GLA_DV = 128
GLA_LOW_RANK = 16
GLA_GATE_NORMALIZER = 16.0
GDN_HEADS = 8
GDN_DK = 128
GDN_DV = 128
SHORT_CONV = 3
D_FF = 5632
FFN_CONV = 3
CHUNK = 64
NORM_EPS = 1e-6

GLA_KW = GLA_HEADS * GLA_DK
GLA_VW = GLA_HEADS * GLA_DV
GDN_KW = GDN_HEADS * GDN_DK
GDN_VW = GDN_HEADS * GDN_DV
GDN_QKV = 2 * GDN_KW + GDN_VW
IN_SIZES = (
    GLA_KW, GLA_KW, GLA_VW, GLA_VW,
    GLA_LOW_RANK, GLA_LOW_RANK,
    GDN_QKV, GDN_VW,
    GDN_HEADS, GDN_HEADS, GDN_HEADS, GDN_HEADS,
    D_MODEL, D_MODEL,
)
N_IN = sum(IN_SIZES)

kernel_name = "hybrid_gla_gdn_convffn_encoder"


def rms_norm(x, g):
    xf = x.astype(jnp.float32)
    y = xf * lax.rsqrt(jnp.mean(xf * xf, axis=-1, keepdims=True) + NORM_EPS) * g.astype(jnp.float32)
    return y.astype(x.dtype)


def depthwise_conv(x, w):
    k = w.shape[0]
    return lax.conv_general_dilated(
        x, w[:, None, :].astype(x.dtype), window_strides=(1,),
        padding=[(k // 2, k // 2)], dimension_numbers=('NWC', 'WIO', 'NWC'),
        feature_group_count=x.shape[-1])


def to_heads(t, n_heads):
    b, s, _ = t.shape
    return t.astype(jnp.float32).reshape(b, s, n_heads, -1).transpose(0, 2, 1, 3)


def flip_seq(t):
    return jnp.flip(t, axis=2)


def l2_norm(t):
    return t * lax.rsqrt(jnp.sum(t * t, axis=-1, keepdims=True) + NORM_EPS)


def gated_head_norm(o, z, g):
    o = o * lax.rsqrt(jnp.mean(o * o, axis=-1, keepdims=True) + NORM_EPS) * g.astype(jnp.float32)
    b, h, s, d = o.shape
    o = o.transpose(0, 2, 1, 3).reshape(b, s, h * d)
    return o * jax.nn.silu(z.astype(jnp.float32))


def gla_scan(q, k, v, log_g):
    b, h, s, dk = q.shape
    dv = v.shape[-1]
    n = s // CHUNK
    q, k, v, log_g = (t.reshape(b, h, n, CHUNK, t.shape[-1]) for t in (q, k, v, log_g))
    G = jnp.cumsum(log_g, axis=3)
    G_last = G[:, :, :, -1:, :]
    q_dec = q * jnp.exp(G)
    k_inv = k * jnp.exp(-G)
    k_tail = k * jnp.exp(G_last - G)
    lower = jnp.tril(jnp.ones((CHUNK, CHUNK), dtype=bool))
    scores = jnp.where(lower, jnp.einsum('bhnid,bhnjd->bhnij', q_dec, k_inv), 0.0)
    o_intra = jnp.einsum('bhnij,bhnjv->bhniv', scores, v)
    chunk_state = jnp.einsum('bhncd,bhncv->bhndv', k_tail, v)
    chunk_decay = jnp.exp(G_last[:, :, :, 0, :])

    def step(state, inp):
        decay, upd = inp
        return state * decay[..., None] + upd, state

    _, s_in = lax.scan(step, jnp.zeros((b, h, dk, dv), q.dtype),
                       (jnp.moveaxis(chunk_decay, 2, 0), jnp.moveaxis(chunk_state, 2, 0)))
    o_inter = jnp.einsum('bhncd,nbhdv->bhncv', q_dec, s_in)
    return (o_intra + o_inter).reshape(b, h, s, dv)


def gdn_scan(q, k, v, g, beta):
    b, h, s, dk = q.shape
    dv = v.shape[-1]
    n = s // CHUNK
    q, k, v = (t.reshape(b, h, n, CHUNK, t.shape[-1]) for t in (q, k, v))
    g, beta = (t.reshape(b, h, n, CHUNK) for t in (g, beta))
    G = jnp.cumsum(g, axis=-1)
    lower_incl = jnp.tril(jnp.ones((CHUNK, CHUNK), dtype=bool))
    lower_strict = jnp.tril(jnp.ones((CHUNK, CHUNK), dtype=bool), k=-1)
    decay = jnp.exp(jnp.where(lower_incl, G[..., :, None] - G[..., None, :], -jnp.inf))
    k_beta = k * beta[..., None]
    v_beta = v * beta[..., None]
    L = jnp.where(lower_strict, jnp.einsum('bhnid,bhnjd->bhnij', k_beta, k), 0.0) * decay
    system = L + jnp.eye(CHUNK, dtype=L.dtype)
    rhs = jnp.concatenate([k_beta * jnp.exp(G)[..., None], v_beta], axis=-1)
    sol = lax.linalg.triangular_solve(system, rhs, left_side=True, lower=True, unit_diagonal=True)
    w, u = sol[..., :dk], sol[..., dk:]
    attn = jnp.einsum('bhnid,bhnjd->bhnij', q, k) * decay
    q_dec = q * jnp.exp(G)[..., None]
    k_tail = k * jnp.exp(G[..., -1:] - G)[..., None]
    chunk_decay = jnp.exp(G[..., -1])

    def step(state, inp):
        w_c, u_c, q_c, k_c, a_c, d_c = inp
        v_new = u_c - jnp.einsum('bhcd,bhdv->bhcv', w_c, state)
        o_c = jnp.einsum('bhcd,bhdv->bhcv', q_c, state) + jnp.einsum('bhij,bhjv->bhiv', a_c, v_new)
        state = state * d_c[..., None, None] + jnp.einsum('bhcd,bhcv->bhdv', k_c, v_new)
        return state, o_c

    xs = tuple(jnp.moveaxis(t, 2, 0) for t in (w, u, q_dec, k_tail, attn, chunk_decay))
    _, o = lax.scan(step, jnp.zeros((b, h, dk, dv), q.dtype), xs)
    return jnp.moveaxis(o, 0, 2).reshape(b, h, s, dv)


def gla_mixer(q, k, v, gate, lr_f, lr_b, dw_f, db_f, dw_b, db_b, norm_g):
    q = to_heads(q, GLA_HEADS) * (GLA_DK ** -0.5)
    k = to_heads(k, GLA_HEADS)
    v = to_heads(v, GLA_HEADS)
    lg_f = to_heads(jax.nn.log_sigmoid(lr_f.astype(jnp.float32) @ dw_f.astype(jnp.float32)
                                       + db_f.astype(jnp.float32)) / GLA_GATE_NORMALIZER, GLA_HEADS)
    lg_b = to_heads(jax.nn.log_sigmoid(lr_b.astype(jnp.float32) @ dw_b.astype(jnp.float32)
                                       + db_b.astype(jnp.float32)) / GLA_GATE_NORMALIZER, GLA_HEADS)
    o = gla_scan(q, k, v, lg_f) + flip_seq(gla_scan(flip_seq(q), flip_seq(k), flip_seq(v), flip_seq(lg_b)))
    return gated_head_norm(o, gate, norm_g)


def gdn_decay(a, a_log, dt_bias):
    g = -jnp.exp(a_log.astype(jnp.float32)) * jax.nn.softplus(a.astype(jnp.float32) + dt_bias.astype(jnp.float32))
    return g.transpose(0, 2, 1)


def gdn_mixer(qkv, z, a_f, a_b, beta_f, beta_b, conv_w, a_log_f, dt_f, a_log_b, dt_b, norm_g):
    qkv = jax.nn.silu(depthwise_conv(qkv.astype(jnp.float32), conv_w))
    q, k, v = jnp.split(qkv, [GDN_KW, 2 * GDN_KW], axis=-1)
    q = l2_norm(to_heads(q, GDN_HEADS)) * (GDN_DK ** -0.5)
    k = l2_norm(to_heads(k, GDN_HEADS))
    v = to_heads(v, GDN_HEADS)
    g_f = gdn_decay(a_f, a_log_f, dt_f)
    g_b = gdn_decay(a_b, a_log_b, dt_b)
    bt_f = jax.nn.sigmoid(beta_f.astype(jnp.float32)).transpose(0, 2, 1)
    bt_b = jax.nn.sigmoid(beta_b.astype(jnp.float32)).transpose(0, 2, 1)
    o = gdn_scan(q, k, v, g_f, bt_f) + flip_seq(
        gdn_scan(flip_seq(q), flip_seq(k), flip_seq(v), flip_seq(g_b), flip_seq(bt_b)))
    return gated_head_norm(o, z, norm_g)


def conv_ffn(h, w_up, conv_w, conv_b, w_down):
    u = depthwise_conv(h @ w_up, conv_w) + conv_b.astype(h.dtype)
    gate, val = jnp.split(u, 2, axis=-1)
    return (jax.nn.silu(gate) * val) @ w_down


def setup_inputs(seed: int = 0) -> dict:
    key = jax.random.key(seed)
    ks = iter(jax.random.split(key, 32))

    def normal(shape, scale):
        return jax.random.normal(next(ks), shape, jnp.float32) * scale

    def gain(shape):
        return 1.0 + normal(shape, 0.02)

    def a_log():
        return jnp.log(jax.random.uniform(next(ks), (DEPTH, GDN_HEADS), jnp.float32, 1.0, 16.0))

    def dt_bias():
        dt = jnp.exp(jax.random.uniform(next(ks), (DEPTH, GDN_HEADS), jnp.float32,
                                        float(np.log(1e-3)), float(np.log(1e-1))))
        return dt + jnp.log(-jnp.expm1(-dt))

    L = DEPTH
    return {
        "x": normal((BATCH, SEQ, D_MODEL), 1.0),
        "norm1_g": gain((L, D_MODEL)),
        "w_in": normal((L, D_MODEL, N_IN), D_MODEL ** -0.5),
        "gla_decay_w_f": normal((L, GLA_LOW_RANK, GLA_KW), GLA_LOW_RANK ** -0.5),
        "gla_decay_b_f": normal((L, GLA_KW), 0.1),
        "gla_decay_w_b": normal((L, GLA_LOW_RANK, GLA_KW), GLA_LOW_RANK ** -0.5),
        "gla_decay_b_b": normal((L, GLA_KW), 0.1),
        "gla_norm_g": gain((L, GLA_DV)),
        "gdn_conv_w": normal((L, SHORT_CONV, GDN_QKV), SHORT_CONV ** -0.5),
        "gdn_a_log_f": a_log(),
        "gdn_dt_bias_f": dt_bias(),
        "gdn_a_log_b": a_log(),
        "gdn_dt_bias_b": dt_bias(),
        "gdn_norm_g": gain((L, GDN_DV)),
        "w_branch_gla": normal((L, GLA_VW, D_MODEL), GLA_VW ** -0.5),
        "w_branch_gdn": normal((L, GDN_VW, D_MODEL), GDN_VW ** -0.5),
        "w_out": normal((L, D_MODEL, D_MODEL), D_MODEL ** -0.5),
        "norm2_g": gain((L, D_MODEL)),
        "w_up": normal((L, D_MODEL, 2 * D_FF), D_MODEL ** -0.5),
        "ffn_conv_w": normal((L, FFN_CONV, 2 * D_FF), FFN_CONV ** -0.5),
        "ffn_conv_b": normal((L, 2 * D_FF), 0.02),
        "w_down": normal((L, D_FF, D_MODEL), D_FF ** -0.5),
        "final_norm_g": gain((D_MODEL,)),
    }


def reference(x, norm1_g, w_in, gla_decay_w_f, gla_decay_b_f, gla_decay_w_b, gla_decay_b_b,
              gla_norm_g, gdn_conv_w, gdn_a_log_f, gdn_dt_bias_f, gdn_a_log_b, gdn_dt_bias_b,
              gdn_norm_g, w_branch_gla, w_branch_gdn, w_out, norm2_g, w_up, ffn_conv_w,
              ffn_conv_b, w_down, final_norm_g):
    split_points = []
    acc = 0
    for size in IN_SIZES[:-1]:
        acc += size
        split_points.append(acc)
    for l in range(DEPTH):
        h = rms_norm(x, norm1_g[l])
        proj = h @ w_in[l]
        (gla_q, gla_k, gla_v, gla_gate, gla_lr_f, gla_lr_b, gdn_qkv, gdn_z,
         gdn_a_f, gdn_a_b, gdn_beta_f, gdn_beta_b, gate_gla, gate_gdn) = jnp.split(proj, split_points, axis=-1)
        y_gla = gla_mixer(gla_q, gla_k, gla_v, gla_gate, gla_lr_f, gla_lr_b,
                          gla_decay_w_f[l], gla_decay_b_f[l], gla_decay_w_b[l], gla_decay_b_b[l],
                          gla_norm_g[l]).astype(x.dtype)
        y_gdn = gdn_mixer(gdn_qkv, gdn_z, gdn_a_f, gdn_a_b, gdn_beta_f, gdn_beta_b, gdn_conv_w[l],
                          gdn_a_log_f[l], gdn_dt_bias_f[l], gdn_a_log_b[l], gdn_dt_bias_b[l],
                          gdn_norm_g[l]).astype(x.dtype)
        merged = (jax.nn.sigmoid(gate_gla) * (y_gla @ w_branch_gla[l])
                  + jax.nn.sigmoid(gate_gdn) * (y_gdn @ w_branch_gdn[l]))
        x = x + merged @ w_out[l]
        h = rms_norm(x, norm2_g[l])
        x = x + conv_ffn(h, w_up[l], ffn_conv_w[l], ffn_conv_b[l], w_down[l])
    return rms_norm(x, final_norm_g)
```

```python
import functools

import jax
import jax.numpy as jnp
from jax import lax
from jax.experimental import pallas as pl
from jax.experimental.pallas import tpu as pltpu

F32 = jnp.float32
BF16 = jnp.bfloat16

D_MODEL = 2048
HEADS = 8
HEAD_DIM = 128
HEAD_W = HEADS * HEAD_DIM
GLA_LOW_RANK = 16
GLA_GATE_NORMALIZER = 16.0
D_FF = 5632
CHUNK = 64
NORM_EPS = 1e-6
LANES = 128

_OFF_LR = 4 * HEAD_W
_OFF_GDN = _OFF_LR + 2 * GLA_LOW_RANK
_OFF_AB = _OFF_GDN + 4 * HEAD_W
_OFF_GATES = _OFF_AB + 4 * HEADS
N_BIG = 8 * HEAD_W + 2 * D_MODEL
_SM_A_F, _SM_A_B, _SM_BETA_F, _SM_BETA_B = 32, 40, 48, 56
_CB_GLA_Q, _CB_GLA_K, _CB_GLA_V, _CB_GLA_GATE = 0, 8, 16, 24
_CB_GDN_Q, _CB_GDN_K, _CB_GDN_V, _CB_GDN_Z = 32, 40, 48, 56

VMEM_LIMIT = 56 * 1024 * 1024


def _dot(a, b):
    return jnp.dot(a, b, preferred_element_type=F32)


def _dot_nt(a, b):
    return lax.dot_general(a, b, (((1,), (1,)), ((), ())), preferred_element_type=F32)


def _dot_tn(a, b):
    return lax.dot_general(a, b, (((0,), (0,)), ((), ())), preferred_element_type=F32)


def _split(x):
    hi = x.astype(BF16)
    lo = (x - hi.astype(F32)).astype(BF16)
    return hi, lo


def _dot_exact_rhs(a_bf, b):
    hi, lo = _split(b)
    return _dot(a_bf, hi) + _dot(a_bf, lo)


def _dot_exact_lhs(a, b_bf):
    hi, lo = _split(a)
    return _dot(hi, b_bf) + _dot(lo, b_bf)


def _dot3(a, b):
    ah, al = _split(a)
    bh, bl = _split(b)
    return _dot(ah, bh) + (_dot(ah, bl) + _dot(al, bh))


def _sigmoid(x):
    return 1.0 / (1.0 + jnp.exp(-x))


def _silu(x):
    return x * _sigmoid(x)


def _softplus(x):
    return jnp.maximum(x, 0.0) + jnp.log1p(jnp.exp(-jnp.abs(x)))


def _rms_scale(x):
    return x * lax.rsqrt(jnp.mean(x * x, axis=-1, keepdims=True) + NORM_EPS)


def _in_proj_kernel(x_ref, g_ref, w_ref, ws_ref, p_ref, sm_ref, hn_ref):
    @pl.when(pl.program_id(1) == 0)
    def _():
        hn = (_rms_scale(x_ref[...]) * g_ref[...]).astype(BF16)
        hn_ref[...] = hn
        sm_ref[...] = _dot(hn, ws_ref[...])

    p_ref[...] = _dot(hn_ref[...], w_ref[...])


def _in_proj(x2, g, w_big, w_small, tm=1024, tn=1024):
    m, d = x2.shape
    n = w_big.shape[1]
    return pl.pallas_call(
        _in_proj_kernel,
        grid=(m // tm, n // tn),
        in_specs=[
            pl.BlockSpec((tm, d), lambda i, j: (i, 0)),
            pl.BlockSpec((1, d), lambda i, j: (0, 0)),
            pl.BlockSpec((d, tn), lambda i, j: (0, j)),
            pl.BlockSpec((d, LANES), lambda i, j: (0, 0)),
        ],
        out_specs=[
            pl.BlockSpec((tm, tn), lambda i, j: (i, j)),
            pl.BlockSpec((tm, LANES), lambda i, j: (i, 0)),
        ],
        out_shape=[
            jax.ShapeDtypeStruct((m, n), F32),
            jax.ShapeDtypeStruct((m, LANES), F32),
        ],
        scratch_shapes=[pltpu.VMEM((tm, d), BF16)],
        compiler_params=pltpu.CompilerParams(
            dimension_semantics=("parallel", "arbitrary"), vmem_limit_bytes=VMEM_LIMIT),
        name="in_proj",
    )(x2, g, w_big, w_small)


def _chunk_masks():
    r = lax.broadcasted_iota(jnp.int32, (CHUNK, CHUNK), 0)
    c = lax.broadcasted_iota(jnp.int32, (CHUNK, CHUNK), 1)
    return r, c


def _gated_head_norm(o, z, g):
    return (_rms_scale(o) * g) * _silu(z)


def _gla_chunk(r, st, q_ref, k_ref, v_ref, lg_ref, o_ref, incl, tri_bf, edge):
    rows = pl.ds(r, CHUNK)
    g_cum = _dot_exact_rhs(tri_bf, lg_ref[rows, :])
    g_edge = g_cum[edge:edge + 1, :]
    q = q_ref[rows, :] * (HEAD_DIM ** -0.5)
    k = k_ref[rows, :]
    v_bf = v_ref[rows, :].astype(BF16)
    q_dec = (q * jnp.exp(g_cum)).astype(BF16)
    k_inv = (k * jnp.exp(-g_cum)).astype(BF16)
    k_tail = (k * jnp.exp(g_edge - g_cum)).astype(BF16)
    scores = jnp.where(incl, _dot_nt(q_dec, k_inv), 0.0).astype(BF16)
    o = _dot(scores, v_bf) + _dot_nt(q_dec, st.astype(BF16))
    o_ref[rows, :] += o
    return st * jnp.exp(g_edge) + _dot_tn(v_bf, k_tail)


def _gla_kernel(q_ref, k_ref, v_ref, gate_ref, sm_ref, dwf_ref, dbf_ref, dwb_ref, dbb_ref, ng_ref,
                y_ref, lgf_ref, lgb_ref, o_ref):
    seq = q_ref.shape[0]
    n_chunks = seq // CHUNK
    sm_bf = sm_ref[...].astype(BF16)

    def log_decay(dw_ref, db_ref):
        pre = _dot(sm_bf, dw_ref[...]) + db_ref[...]
        return -_softplus(-pre) / GLA_GATE_NORMALIZER

    lgf_ref[...] = log_decay(dwf_ref, dbf_ref)
    lgb_ref[...] = log_decay(dwb_ref, dbb_ref)
    o_ref[...] = jnp.zeros_like(o_ref)

    r, c = _chunk_masks()
    lower, upper = r >= c, r <= c
    tril_bf = lower.astype(F32).astype(BF16)
    triu_bf = upper.astype(F32).astype(BF16)

    def body(t, carry):
        st_f, st_b = carry
        rf = pl.multiple_of(t * CHUNK, CHUNK)
        rb = pl.multiple_of((n_chunks - 1 - t) * CHUNK, CHUNK)
        st_f = _gla_chunk(rf, st_f, q_ref, k_ref, v_ref, lgf_ref, o_ref, lower, tril_bf, CHUNK - 1)
        st_b = _gla_chunk(rb, st_b, q_ref, k_ref, v_ref, lgb_ref, o_ref, upper, triu_bf, 0)
        return st_f, st_b

    zero = jnp.zeros((HEAD_DIM, HEAD_DIM), F32)
    lax.fori_loop(0, n_chunks, body, (zero, zero))
    y_ref[...] = _gated_head_norm(o_ref[...], gate_ref[...], ng_ref[...]).astype(y_ref.dtype)


def _gla(p, sm, dw_f, db_f, dw_b, db_b, norm_g, batch, seq):
    def col(base):
        return pl.BlockSpec((seq, LANES), lambda b, h: (b, base + h))

    head_vec = pl.BlockSpec((1, LANES), lambda b, h: (0, h))
    head_mat = pl.BlockSpec((LANES, LANES), lambda b, h: (0, h))
    shared_vec = pl.BlockSpec((1, LANES), lambda b, h: (0, 0))
    return pl.pallas_call(
        _gla_kernel,
        grid=(batch, HEADS),
        in_specs=[col(_CB_GLA_Q), col(_CB_GLA_K), col(_CB_GLA_V), col(_CB_GLA_GATE),
                  pl.BlockSpec((seq, LANES), lambda b, h: (b, 0)),
                  head_mat, head_vec, head_mat, head_vec, shared_vec],
        out_specs=pl.BlockSpec((seq, LANES), lambda b, h: (b, h)),
        out_shape=jax.ShapeDtypeStruct((batch * seq, HEAD_W), BF16),
        scratch_shapes=[pltpu.VMEM((seq, LANES), F32)] * 3,
        compiler_params=pltpu.CompilerParams(
            dimension_semantics=("parallel", "arbitrary"), vmem_limit_bytes=VMEM_LIMIT),
        name="gla",
    )(p, p, p, p, sm, dw_f, db_f, dw_b, db_b, norm_g)


def _gdn_chunk(r, state, qs_ref, ks_ref, vs_ref, g_ref, beta_ref, o_ref, incl, strict, tri_bf, eye, edge):
    rows = pl.ds(r, CHUNK)
    q = qs_ref[rows, :]
    k = ks_ref[rows, :]
    v = vs_ref[rows, :]
    beta = beta_ref[rows, :]
    g_cum = _dot_exact_rhs(tri_bf, g_ref[rows, :])
    g_edge = g_cum[edge:edge + 1, :]
    g_row = g_cum.T[:CHUNK, :]
    diff = g_cum[:, :CHUNK] - g_row
    decay = jnp.where(incl, jnp.exp(jnp.where(incl, diff, 0.0)), 0.0)

    k_bf = k.astype(BF16)
    k_beta = k * beta
    kk = _dot_nt(k_beta.astype(BF16), k_bf)
    a = -(jnp.where(strict, kk, 0.0) * decay)
    t_inv = eye + a
    p = a
    for _ in range(5):
        p = _dot3(p, p)
        t_inv = t_inv + _dot3(t_inv, p)
    t_bf = t_inv.astype(BF16)
    e_g = jnp.exp(g_cum)
    w = _dot(t_bf, (k_beta * e_g).astype(BF16))
    u = _dot(t_bf, (v * beta).astype(BF16))
    attn = (_dot_nt(q.astype(BF16), k_bf) * decay).astype(BF16)
    q_dec = (q * e_g).astype(BF16)
    k_tail = (k * jnp.exp(g_edge - g_cum)).astype(BF16)

    s_bf = state.astype(BF16)
    v_new = (u - _dot(w.astype(BF16), s_bf)).astype(BF16)
    o_ref[rows, :] += _dot(q_dec, s_bf) + _dot(attn, v_new)
    return state * jnp.exp(g_edge) + _dot_tn(k_tail, v_new)


def _gdn_kernel(alog_ref, dtb_ref, q_ref, k_ref, v_ref, z_ref, sm_ref, cq_ref, ck_ref, cv_ref, ng_ref,
                y_ref, qs_ref, ks_ref, vs_ref, gf_ref, gb_ref, bf_ref, bb_ref, o_ref):
    h = pl.program_id(1)
    seq = q_ref.shape[0]
    n_chunks = seq // CHUNK
    pos = lax.broadcasted_iota(jnp.int32, (seq, 1), 0)

    def conv_silu(x_ref, c_ref):
        x = x_ref[...]
        cw = c_ref[...]
        x_prev = jnp.where(pos == 0, 0.0, pltpu.roll(x, 1, 0))
        x_next = jnp.where(pos == seq - 1, 0.0, pltpu.roll(x, seq - 1, 0))
        return _silu(x_prev * cw[0:1, :] + x * cw[1:2, :] + x_next * cw[2:3, :])

    def l2_norm(t):
        return t * lax.rsqrt(jnp.sum(t * t, axis=-1, keepdims=True) + NORM_EPS)

    qs_ref[...] = l2_norm(conv_silu(q_ref, cq_ref)) * (HEAD_DIM ** -0.5)
    ks_ref[...] = l2_norm(conv_silu(k_ref, ck_ref))
    vs_ref[...] = conv_silu(v_ref, cv_ref)

    sm_hi, sm_lo = _split(sm_ref[...])
    sel_row = lax.broadcasted_iota(jnp.int32, (LANES, LANES), 0)

    def head_col(base):
        sel = (sel_row == base + h).astype(F32).astype(BF16)
        return _dot(sm_hi, sel) + _dot(sm_lo, sel)

    def log_decay(base, d):
        rate = jnp.exp(jnp.full((1, LANES), alog_ref[d, h], F32))
        return -rate * _softplus(head_col(base) + dtb_ref[d, h])

    gf_ref[...] = log_decay(_SM_A_F, 0)
    gb_ref[...] = log_decay(_SM_A_B, 1)
    bf_ref[...] = _sigmoid(head_col(_SM_BETA_F))
    bb_ref[...] = _sigmoid(head_col(_SM_BETA_B))
    o_ref[...] = jnp.zeros_like(o_ref)

    r, c = _chunk_masks()
    lower, upper = r >= c, r <= c
    tril_bf = lower.astype(F32).astype(BF16)
    triu_bf = upper.astype(F32).astype(BF16)
    eye = (r == c).astype(F32)

    def body(t, carry):
        s_f, s_b = carry
        rf = pl.multiple_of(t * CHUNK, CHUNK)
        rb = pl.multiple_of((n_chunks - 1 - t) * CHUNK, CHUNK)
        s_f = _gdn_chunk(rf, s_f, qs_ref, ks_ref, vs_ref, gf_ref, bf_ref, o_ref,
                         lower, r > c, tril_bf, eye, CHUNK - 1)
        s_b = _gdn_chunk(rb, s_b, qs_ref, ks_ref, vs_ref, gb_ref, bb_ref, o_ref,
                         upper, r < c, triu_bf, eye, 0)
        return s_f, s_b

    zero = jnp.zeros((HEAD_DIM, HEAD_DIM), F32)
    lax.fori_loop(0, n_chunks, body, (zero, zero))
    y_ref[...] = _gated_head_norm(o_ref[...], z_ref[...], ng_ref[...]).astype(y_ref.dtype)


def _gdn(p, sm, a_log, dt_bias, conv_w, norm_g, batch, seq):
    def col(base):
        return pl.BlockSpec((seq, LANES), lambda b, h: (b, base + h))

    def conv_col(base):
        return pl.BlockSpec((3, LANES), lambda b, h: (0, base + h))

    smem = pl.BlockSpec(memory_space=pltpu.SMEM)
    return pl.pallas_call(
        _gdn_kernel,
        grid=(batch, HEADS),
        in_specs=[smem, smem,
                  col(_CB_GDN_Q), col(_CB_GDN_K), col(_CB_GDN_V), col(_CB_GDN_Z),
                  pl.BlockSpec((seq, LANES), lambda b, h: (b, 0)),
                  conv_col(0), conv_col(HEADS), conv_col(2 * HEADS),
                  pl.BlockSpec((1, LANES), lambda b, h: (0, 0))],
        out_specs=pl.BlockSpec((seq, LANES), lambda b, h: (b, h)),
        out_shape=jax.ShapeDtypeStruct((batch * seq, HEAD_W), BF16),
        scratch_shapes=[pltpu.VMEM((seq, LANES), F32)] * 8,
        compiler_params=pltpu.CompilerParams(
            dimension_semantics=("parallel", "arbitrary"), vmem_limit_bytes=VMEM_LIMIT),
        name="gdn",
    )(a_log, dt_bias, p, p, p, p, sm, conv_w, conv_w, conv_w, norm_g)


def _merge_out_kernel(yg_ref, yd_ref, gg_ref, gd_ref, x_ref, wbg_ref, wbd_ref, wo_ref, n2_ref,
                      x1_ref, h2_ref):
    merged = (_sigmoid(gg_ref[...]) * _dot(yg_ref[...], wbg_ref[...])
              + _sigmoid(gd_ref[...]) * _dot(yd_ref[...], wbd_ref[...]))
    x1 = x_ref[...] + _dot(merged.astype(BF16), wo_ref[...])
    x1_ref[...] = x1
    h2_ref[...] = (_rms_scale(x1) * n2_ref[...]).astype(h2_ref.dtype)


def _merge_out(y_gla, y_gdn, p, x2, w_bg, w_bd, w_o, n2, tm=256):
    m, d = x2.shape
    gate_cb = (8 * HEAD_W) // d

    def resident(shape):
        return pl.BlockSpec(shape, lambda i: (0, 0), pipeline_mode=pl.Buffered(1))

    return pl.pallas_call(
        _merge_out_kernel,
        grid=(m // tm,),
        in_specs=[
            pl.BlockSpec((tm, HEAD_W), lambda i: (i, 0)),
            pl.BlockSpec((tm, HEAD_W), lambda i: (i, 0)),
            pl.BlockSpec((tm, d), lambda i: (i, gate_cb)),
            pl.BlockSpec((tm, d), lambda i: (i, gate_cb + 1)),
            pl.BlockSpec((tm, d), lambda i: (i, 0)),
            resident((HEAD_W, d)), resident((HEAD_W, d)), resident((d, d)), resident((1, d)),
        ],
        out_specs=[pl.BlockSpec((tm, d), lambda i: (i, 0)), pl.BlockSpec((tm, d), lambda i: (i, 0))],
        out_shape=[jax.ShapeDtypeStruct((m, d), F32), jax.ShapeDtypeStruct((m, d), BF16)],
        compiler_params=pltpu.CompilerParams(
            dimension_semantics=("parallel",), vmem_limit_bytes=VMEM_LIMIT),
        name="merge_out",
    )(y_gla, y_gdn, p, p, x2, w_bg, w_bd, w_o, n2)


def _ffn_up_kernel(h_ref, wg_ref, wv_ref, cg_ref, cv_ref, bg_ref, bv_ref, act_ref):
    h = h_ref[...]
    seq = h.shape[0]
    pos = lax.broadcasted_iota(jnp.int32, (seq, 1), 0)

    def branch(w_ref, c_ref, b_ref):
        u = _dot(h, w_ref[...])
        cw = c_ref[...]
        u_prev = jnp.where(pos == 0, 0.0, pltpu.roll(u, 1, 0))
        u_next = jnp.where(pos == seq - 1, 0.0, pltpu.roll(u, seq - 1, 0))
        return u_prev * cw[0:1, :] + u * cw[1:2, :] + u_next * cw[2:3, :] + b_ref[...]

    gate = branch(wg_ref, cg_ref, bg_ref)
    val = branch(wv_ref, cv_ref, bv_ref)
    act_ref[...] = (_silu(gate) * val).astype(act_ref.dtype)


def _ffn_up(h2, w_up, conv_w, conv_b, batch, seq, tn=256):
    d = h2.shape[1]
    nj = D_FF // tn
    return pl.pallas_call(
        _ffn_up_kernel,
        grid=(batch, nj),
        in_specs=[
            pl.BlockSpec((seq, d), lambda b, j: (b, 0)),
            pl.BlockSpec((d, tn), lambda b, j: (0, j)),
            pl.BlockSpec((d, tn), lambda b, j: (0, nj + j)),
            pl.BlockSpec((3, tn), lambda b, j: (0, j)),
            pl.BlockSpec((3, tn), lambda b, j: (0, nj + j)),
            pl.BlockSpec((1, tn), lambda b, j: (0, j)),
            pl.BlockSpec((1, tn), lambda b, j: (0, nj + j)),
        ],
        out_specs=pl.BlockSpec((seq, tn), lambda b, j: (b, j)),
        out_shape=jax.ShapeDtypeStruct((batch * seq, D_FF), BF16),
        compiler_params=pltpu.CompilerParams(
            dimension_semantics=("parallel", "arbitrary"), vmem_limit_bytes=VMEM_LIMIT),
        name="ffn_up",
    )(h2, w_up, w_up, conv_w, conv_w, conv_b, conv_b)


def _ffn_down_kernel(act_ref, w_ref, x1_ref, g_ref, out_ref, acc_ref, *, final_norm):
    kk = pl.program_id(1)

    @pl.when(kk == 0)
    def _():
        acc_ref[...] = x1_ref[...]

    acc_ref[...] += _dot(act_ref[...], w_ref[...])

    @pl.when(kk == pl.num_programs(1) - 1)
    def _():
        x2 = acc_ref[...]
        out_ref[...] = _rms_scale(x2) * g_ref[...] if final_norm else x2


def _ffn_down(act, w_down, x1, g, final_norm, tm=512, tk=1408):
    m, d = x1.shape
    return pl.pallas_call(
        functools.partial(_ffn_down_kernel, final_norm=final_norm),
        grid=(m // tm, D_FF // tk),
        in_specs=[
            pl.BlockSpec((tm, tk), lambda i, k: (i, k)),
            pl.BlockSpec((tk, d), lambda i, k: (k, 0)),
            pl.BlockSpec((tm, d), lambda i, k: (i, 0)),
            pl.BlockSpec((1, d), lambda i, k: (0, 0)),
        ],
        out_specs=pl.BlockSpec((tm, d), lambda i, k: (i, 0)),
        out_shape=jax.ShapeDtypeStruct((m, d), F32),
        scratch_shapes=[pltpu.VMEM((tm, d), F32)],
        compiler_params=pltpu.CompilerParams(
            dimension_semantics=("parallel", "arbitrary"), vmem_limit_bytes=VMEM_LIMIT),
        name="ffn_down",
    )(act, w_down, x1, g)


def _pad_rows(w, row0, rows):
    return jnp.zeros((rows, w.shape[1]), w.dtype).at[row0:row0 + w.shape[0]].set(w)


def kernel(x, norm1_g, w_in, gla_decay_w_f, gla_decay_b_f, gla_decay_w_b, gla_decay_b_b, gla_norm_g,
           gdn_conv_w, gdn_a_log_f, gdn_dt_bias_f, gdn_a_log_b, gdn_dt_bias_b, gdn_norm_g,
           w_branch_gla, w_branch_gdn, w_out, norm2_g, w_up, ffn_conv_w, ffn_conv_b, w_down,
           final_norm_g):
    batch, seq, d = x.shape
    depth = w_in.shape[0]
    x2 = x.reshape(batch * seq, d)
    for l in range(depth):
        wl = w_in[l]
        w_big = jnp.concatenate(
            [wl[:, :_OFF_LR], wl[:, _OFF_GDN:_OFF_AB], wl[:, _OFF_GATES:]], axis=1).astype(BF16)
        w_small = jnp.concatenate(
            [wl[:, _OFF_LR:_OFF_GDN], wl[:, _OFF_AB:_OFF_GATES],
             jnp.zeros((d, LANES - 2 * GLA_LOW_RANK - 4 * HEADS), wl.dtype)], axis=1).astype(BF16)
        p, sm = _in_proj(x2, norm1_g[l][None, :], w_big, w_small)

        y_gla = _gla(p, sm,
                     _pad_rows(gla_decay_w_f[l], 0, LANES).astype(BF16), gla_decay_b_f[l][None, :],
                     _pad_rows(gla_decay_w_b[l], GLA_LOW_RANK, LANES).astype(BF16), gla_decay_b_b[l][None, :],
                     gla_norm_g[l][None, :], batch, seq)
        y_gdn = _gdn(p, sm,
                     jnp.stack([gdn_a_log_f[l], gdn_a_log_b[l]]),
                     jnp.stack([gdn_dt_bias_f[l], gdn_dt_bias_b[l]]),
                     gdn_conv_w[l], gdn_norm_g[l][None, :], batch, seq)

        x2, h2 = _merge_out(y_gla, y_gdn, p, x2,
                            w_branch_gla[l].astype(BF16), w_branch_gdn[l].astype(BF16),
                            w_out[l].astype(BF16), norm2_g[l][None, :])
        act = _ffn_up(h2, w_up[l].astype(BF16), ffn_conv_w[l], ffn_conv_b[l][None, :], batch, seq)
        x2 = _ffn_down(act, w_down[l].astype(BF16), x2, final_norm_g[None, :], l == depth - 1)
    return x2.reshape(batch, seq, d)
```

```python
import functools

import jax
import jax.numpy as jnp
from jax import lax
from jax.experimental import pallas as pl
from jax.experimental.pallas import tpu as pltpu

F32 = jnp.float32
BF16 = jnp.bfloat16

D_MODEL = 2048
HEADS = 8
HEAD_DIM = 128
HEAD_W = HEADS * HEAD_DIM
GLA_LOW_RANK = 16
GLA_GATE_NORMALIZER = 16.0
D_FF = 5632
CHUNK = 64
NORM_EPS = 1e-6
LANES = 128
CUM_ROWS = 256
GLA_UNROLL = 4
GDN_GROUP = 4
NQ_ROWS = HEAD_DIM + CHUNK

_OFF_LR = 4 * HEAD_W
_OFF_GDN = _OFF_LR + 2 * GLA_LOW_RANK
_OFF_AB = _OFF_GDN + 4 * HEAD_W
_OFF_GATES = _OFF_AB + 4 * HEADS
N_BIG = 8 * HEAD_W + 2 * D_MODEL
_SM_A_F, _SM_A_B, _SM_BETA_F, _SM_BETA_B = 32, 40, 48, 56
_CB_GLA_Q, _CB_GLA_K, _CB_GLA_V, _CB_GLA_GATE = 0, 8, 16, 24
_CB_GDN_Q, _CB_GDN_K, _CB_GDN_V, _CB_GDN_Z = 32, 40, 48, 56

VMEM_LIMIT = 56 * 1024 * 1024


def _dot(a, b):
    return jnp.dot(a, b, preferred_element_type=F32)


def _dot_nt(a, b):
    return lax.dot_general(a, b, (((1,), (1,)), ((), ())), preferred_element_type=F32)


def _dot_tn(a, b):
    return lax.dot_general(a, b, (((0,), (0,)), ((), ())), preferred_element_type=F32)


def _split(x):
    hi = x.astype(BF16)
    lo = (x - hi.astype(F32)).astype(BF16)
    return hi, lo


def _dot_exact_rhs(a_bf, b):
    hi, lo = _split(b)
    return _dot(a_bf, hi) + _dot(a_bf, lo)


def _dot_exact_lhs(a, b_bf):
    hi, lo = _split(a)
    return _dot(hi, b_bf) + _dot(lo, b_bf)


def _dot3(a, b):
    ah, al = _split(a)
    bh, bl = _split(b)
    return _dot(ah, bh) + (_dot(ah, bl) + _dot(al, bh))


def _sigmoid(x):
    return 1.0 / (1.0 + jnp.exp(-x))


def _silu(x):
    return x * _sigmoid(x)


def _softplus(x):
    return jnp.maximum(x, 0.0) + jnp.log1p(jnp.exp(-jnp.abs(x)))


def _rms_scale(x):
    return x * lax.rsqrt(jnp.mean(x * x, axis=-1, keepdims=True) + NORM_EPS)


def _gdn_gates(sm, alog_row, dtb_row):
    tm = sm.shape[0]
    col = lax.broadcasted_iota(jnp.int32, (1, LANES), 1)
    is_g = (col >= _SM_A_F) & (col < _SM_BETA_F)
    is_fwd = col < _SM_A_B
    is_beta = (col >= _SM_BETA_F) & (col < _SM_BETA_B + HEADS)
    g = jnp.where(is_g, -jnp.exp(alog_row) * _softplus(sm + dtb_row), 0.0)
    r = lax.broadcasted_iota(jnp.int32, (CUM_ROWS, CUM_ROWS), 0)
    c = lax.broadcasted_iota(jnp.int32, (CUM_ROWS, CUM_ROWS), 1)
    same_chunk = (r // CHUNK) == (c // CHUNK)
    tril_bf = (same_chunk & (r >= c)).astype(F32).astype(BF16)
    triu_bf = (same_chunk & (r <= c)).astype(F32).astype(BF16)
    slabs = []
    for s in range(tm // CUM_ROWS):
        hi, lo = _split(g[s * CUM_ROWS:(s + 1) * CUM_ROWS])
        prefix = _dot(tril_bf, hi) + _dot(tril_bf, lo)
        suffix = _dot(triu_bf, hi) + _dot(triu_bf, lo)
        slabs.append(jnp.where(is_fwd, prefix, suffix))
    g_cum = jnp.concatenate(slabs, axis=0)
    return jnp.where(is_g, g_cum, jnp.where(is_beta, _sigmoid(sm), sm))


def _in_proj_kernel(x_ref, g_ref, w_ref, ws_ref, alog_ref, dtb_ref, p_ref, sm_ref, hn_ref):
    @pl.when(pl.program_id(1) == 0)
    def _():
        hn = (_rms_scale(x_ref[...]) * g_ref[...]).astype(BF16)
        hn_ref[...] = hn
        sm_ref[...] = _gdn_gates(_dot(hn, ws_ref[...]), alog_ref[...], dtb_ref[...])

    p_ref[...] = _dot(hn_ref[...], w_ref[...])


def _in_proj(x2, g, w_big, w_small, alog_row, dtb_row, tm=1024, tn=1024):
    m, d = x2.shape
    n = w_big.shape[1]
    return pl.pallas_call(
        _in_proj_kernel,
        grid=(m // tm, n // tn),
        in_specs=[
            pl.BlockSpec((tm, d), lambda i, j: (i, 0)),
            pl.BlockSpec((1, d), lambda i, j: (0, 0)),
            pl.BlockSpec((d, tn), lambda i, j: (0, j)),
            pl.BlockSpec((d, LANES), lambda i, j: (0, 0)),
            pl.BlockSpec((1, LANES), lambda i, j: (0, 0)),
            pl.BlockSpec((1, LANES), lambda i, j: (0, 0)),
        ],
        out_specs=[
            pl.BlockSpec((tm, tn), lambda i, j: (i, j)),
            pl.BlockSpec((tm, LANES), lambda i, j: (i, 0)),
        ],
        out_shape=[
            jax.ShapeDtypeStruct((m, n), F32),
            jax.ShapeDtypeStruct((m, LANES), F32),
        ],
        scratch_shapes=[pltpu.VMEM((tm, d), BF16)],
        compiler_params=pltpu.CompilerParams(
            dimension_semantics=("parallel", "arbitrary"), vmem_limit_bytes=VMEM_LIMIT),
        name="in_proj",
    )(x2, g, w_big, w_small, alog_row, dtb_row)


def _chunk_masks():
    r = lax.broadcasted_iota(jnp.int32, (CHUNK, CHUNK), 0)
    c = lax.broadcasted_iota(jnp.int32, (CHUNK, CHUNK), 1)
    return r, c


def _gated_head_norm(o, z, g):
    return (_rms_scale(o) * g) * _silu(z)


def _gla_chunks(chunks, states, q_ref, k_ref, v_ref, dirs, o_ref):
    pairs = []
    for u in range(len(chunks[0])):
        for d, (lg_ref, incl, tri_bf, edge) in enumerate(dirs):
            rows = pl.ds(pl.multiple_of(chunks[d][u] * CHUNK, CHUNK), CHUNK)
            hi, lo = _split(lg_ref[rows, :])
            pairs.append(dict(d=d, rows=rows, incl=incl, edge=edge, hi=hi, lo=lo, tri=tri_bf))
    for p in pairs:
        p["g_cum"] = _dot(p["tri"], p["hi"]) + _dot(p["tri"], p["lo"])
    for p in pairs:
        g_cum = p["g_cum"]
        g_edge = g_cum[p["edge"]:p["edge"] + 1, :]
        k = k_ref[p["rows"], :]
        p["v_bf"] = _bf(v_ref[p["rows"], :])
        p["q_dec"] = _bf(q_ref[p["rows"], :] * (HEAD_DIM ** -0.5) * jnp.exp(g_cum))
        p["decay"] = jnp.exp(g_edge)
        p["scores"] = _dot_nt(p["q_dec"], _bf(k * jnp.exp(-g_cum)))
        p["upd"] = _dot_tn(p["v_bf"], _bf(k * jnp.exp(g_edge - g_cum)))
    for p in pairs:
        p["o"] = _dot(_bf(jnp.where(p["incl"], p["scores"], 0.0)), p["v_bf"])
    states = list(states)
    for p in pairs:
        st = states[p["d"]]
        o_ref[p["rows"], :] += p["o"] + _dot_nt(p["q_dec"], _bf(st))
        states[p["d"]] = st * p["decay"] + p["upd"]
    return tuple(states)


def _gla_kernel(q_ref, k_ref, v_ref, gate_ref, sm_ref, dwf_ref, dbf_ref, dwb_ref, dbb_ref, ng_ref,
                y_ref, lgf_ref, lgb_ref, o_ref):
    seq = q_ref.shape[0]
    n_chunks = seq // CHUNK
    sm_bf = sm_ref[...].astype(BF16)

    def log_decay(dw_ref, db_ref):
        pre = _dot(sm_bf, dw_ref[...]) + db_ref[...]
        return -_softplus(-pre) / GLA_GATE_NORMALIZER

    lgf_ref[...] = log_decay(dwf_ref, dbf_ref)
    lgb_ref[...] = log_decay(dwb_ref, dbb_ref)
    o_ref[...] = jnp.zeros_like(o_ref)

    r, c = _chunk_masks()
    lower, upper = r >= c, r <= c
    tril_bf = lower.astype(F32).astype(BF16)
    triu_bf = upper.astype(F32).astype(BF16)

    dirs = ((lgf_ref, lower, tril_bf, CHUNK - 1), (lgb_ref, upper, triu_bf, 0))

    def body(t, carry):
        fwd = [t * GLA_UNROLL + u for u in range(GLA_UNROLL)]
        bwd = [n_chunks - 1 - n for n in fwd]
        return _gla_chunks((fwd, bwd), carry, q_ref, k_ref, v_ref, dirs, o_ref)

    zero = jnp.zeros((HEAD_DIM, HEAD_DIM), F32)
    lax.fori_loop(0, n_chunks // GLA_UNROLL, body, (zero, zero))
    y_ref[...] = _gated_head_norm(o_ref[...], gate_ref[...], ng_ref[...]).astype(y_ref.dtype)


def _gla(p, sm, dw_f, db_f, dw_b, db_b, norm_g, batch, seq):
    def col(base):
        return pl.BlockSpec((seq, LANES), lambda b, h: (b, base + h))

    head_vec = pl.BlockSpec((1, LANES), lambda b, h: (0, h))
    head_mat = pl.BlockSpec((LANES, LANES), lambda b, h: (0, h))
    shared_vec = pl.BlockSpec((1, LANES), lambda b, h: (0, 0))
    return pl.pallas_call(
        _gla_kernel,
        grid=(batch, HEADS),
        in_specs=[col(_CB_GLA_Q), col(_CB_GLA_K), col(_CB_GLA_V), col(_CB_GLA_GATE),
                  pl.BlockSpec((seq, LANES), lambda b, h: (b, 0)),
                  head_mat, head_vec, head_mat, head_vec, shared_vec],
        out_specs=pl.BlockSpec((seq, LANES), lambda b, h: (b, h)),
        out_shape=jax.ShapeDtypeStruct((batch * seq, HEAD_W), BF16),
        scratch_shapes=[pltpu.VMEM((seq, LANES), F32)] * 3,
        compiler_params=pltpu.CompilerParams(
            dimension_semantics=("parallel", "arbitrary"), vmem_limit_bytes=VMEM_LIMIT),
        name="gla",
    )(p, p, p, p, sm, dw_f, db_f, dw_b, db_b, norm_g)


def _bf(x):
    return x.astype(BF16)


def _unit_tri_inverses(l_mats, r, c, eye):
    blk16 = (r // 16) == (c // 16)
    blk32 = (r // 32) == (c // 32)
    a = [-jnp.where(blk16, l, 0.0) for l in l_mats]
    a_bf = [_bf(x) for x in a]
    p2_bf = [_bf(_dot(x, x)) for x in a_bf]
    t = [eye + x for x in a]
    t = [x + _dot(_bf(x), p) for x, p in zip(t, p2_bf)]
    p4_bf = [_bf(_dot(p, p)) for p in p2_bf]
    t = [x + _dot(_bf(x), p) for x, p in zip(t, p4_bf)]
    p8_bf = [_bf(_dot(p, p)) for p in p4_bf]
    t = [x + _dot(_bf(x), p) for x, p in zip(t, p8_bf)]
    for off_diag in (blk32 & ~blk16, ~blk32):
        t_bf = [_bf(x) for x in t]
        e_t = [_bf(_dot(_bf(jnp.where(off_diag, l, 0.0)), x)) for l, x in zip(l_mats, t_bf)]
        t = [x - _dot(xb, e) for x, xb, e in zip(t, t_bf, e_t)]
    return t


def _gdn_prepare(chunks, qs_ref, ks_ref, vs_ref, dirs, o_ref, nq_ref, b_ref, d_ref, r, c, eye, n_chunks):
    per_chunk = []
    for n in chunks:
        rows = pl.ds(pl.multiple_of(n * CHUNK, CHUNK), CHUNK)
        q = qs_ref[rows, :]
        k = ks_ref[rows, :]
        k_bf = _bf(k)
        per_chunk.append(dict(n=n, rows=rows, q=q, k=k, v=vs_ref[rows, :], k_t=k.T,
                              kk=_dot_nt(k_bf, k_bf), qk=_dot_nt(_bf(q), k_bf)))
    pairs = []
    for ch in per_chunk:
        for d, (g_ref, beta_ref, incl, strict, edge) in enumerate(dirs):
            g_cum = g_ref[ch["rows"], :]
            beta = beta_ref[ch["rows"], :]
            g_cum_t = g_cum.T
            diff = g_cum[:, :CHUNK] - g_cum_t[:CHUNK, :]
            decay = jnp.where(incl, jnp.exp(jnp.where(incl, diff, 0.0)), 0.0)
            l_mat = jnp.where(strict, ch["kk"], 0.0) * decay * beta[:, :CHUNK]
            e_g = jnp.exp(g_cum)
            g_edge = g_cum[edge:edge + 1, :]
            pairs.append(dict(
                ch=ch, d=d, l_mat=l_mat, e_g=e_g, g_edge=g_edge,
                rhs=_bf(jnp.concatenate([ch["k"] * (beta * e_g), ch["v"] * beta], axis=1)),
                attn=_bf(ch["qk"] * decay),
                k_tail_t=_bf(ch["k_t"] * jnp.exp(g_edge[:, :CHUNK] - g_cum_t))))
    t_inv = _unit_tri_inverses([p["l_mat"] for p in pairs], r, c, eye)
    wu_bf = [_bf(_dot(_bf(t), p["rhs"])) for t, p in zip(t_inv, pairs)]
    aw = [_dot(p["attn"], x) for p, x in zip(pairs, wu_bf)]
    kw = [_dot(p["k_tail_t"], x) for p, x in zip(pairs, wu_bf)]
    for p, aw_p, kw_p in zip(pairs, aw, kw):
        ch = p["ch"]
        slot = p["d"] * n_chunks + ch["n"]
        nq_rows = pl.multiple_of(slot * NQ_ROWS, NQ_ROWS)
        nq_ref[pl.ds(nq_rows, HEAD_DIM), :] = _bf(-kw_p[:, :LANES])
        nq_ref[pl.ds(nq_rows + HEAD_DIM, CHUNK), :] = _bf(ch["q"] * p["e_g"] - aw_p[:, :LANES])
        b_ref[pl.ds(pl.multiple_of(slot * HEAD_DIM, HEAD_DIM), HEAD_DIM), :] = kw_p[:, LANES:]
        d_ref[pl.ds(slot, 1), :] = jnp.exp(p["g_edge"])
        o_ref[ch["rows"], :] += aw_p[:, LANES:]


def _gdn_step(n, d, state, o_ref, nq_ref, b_ref, d_ref, n_chunks):
    slot = d * n_chunks + n
    nq = nq_ref[pl.ds(pl.multiple_of(slot * NQ_ROWS, NQ_ROWS), NQ_ROWS), :]
    res = _dot(nq, _bf(state))
    o_ref[pl.ds(pl.multiple_of(n * CHUNK, CHUNK), CHUNK), :] += res[HEAD_DIM:, :]
    b_mat = b_ref[pl.ds(pl.multiple_of(slot * HEAD_DIM, HEAD_DIM), HEAD_DIM), :]
    return state * d_ref[pl.ds(slot, 1), :] + (res[:HEAD_DIM, :] + b_mat)


def _gdn_kernel(q_ref, k_ref, v_ref, z_ref, sm_ref, cq_ref, ck_ref, cv_ref, ng_ref,
                y_ref, qs_ref, ks_ref, vs_ref, gf_ref, gb_ref, bf_ref, bb_ref, o_ref, nq_ref, b_ref, d_ref):
    h = pl.program_id(1)
    seq = q_ref.shape[0]
    n_chunks = seq // CHUNK
    pos = lax.broadcasted_iota(jnp.int32, (seq, 1), 0)

    def conv_silu(x_ref, c_ref):
        x = x_ref[...]
        cw = c_ref[...]
        x_prev = jnp.where(pos == 0, 0.0, pltpu.roll(x, 1, 0))
        x_next = jnp.where(pos == seq - 1, 0.0, pltpu.roll(x, seq - 1, 0))
        return _silu(x_prev * cw[0:1, :] + x * cw[1:2, :] + x_next * cw[2:3, :])

    def l2_norm(t):
        return t * lax.rsqrt(jnp.sum(t * t, axis=-1, keepdims=True) + NORM_EPS)

    qs_ref[...] = l2_norm(conv_silu(q_ref, cq_ref)) * (HEAD_DIM ** -0.5)
    ks_ref[...] = l2_norm(conv_silu(k_ref, ck_ref))
    vs_ref[...] = conv_silu(v_ref, cv_ref)

    sm_hi, sm_lo = _split(sm_ref[...])
    sel_row = lax.broadcasted_iota(jnp.int32, (LANES, 4 * LANES), 0)
    sel_grp = lax.broadcasted_iota(jnp.int32, (LANES, 4 * LANES), 1) // LANES
    sel = (sel_row == _SM_A_F + HEADS * sel_grp + h).astype(F32).astype(BF16)
    cols = _dot(sm_hi, sel) + _dot(sm_lo, sel)
    for i, ref in enumerate((gf_ref, gb_ref, bf_ref, bb_ref)):
        ref[...] = cols[:, i * LANES:(i + 1) * LANES]
    o_ref[...] = jnp.zeros_like(o_ref)

    r, c = _chunk_masks()
    eye = (r == c).astype(F32)
    dirs = ((gf_ref, bf_ref, r >= c, r > c, CHUNK - 1),
            (gb_ref, bb_ref, r <= c, r < c, 0))

    def prepare(t, carry):
        _gdn_prepare([t * GDN_GROUP + u for u in range(GDN_GROUP)], qs_ref, ks_ref, vs_ref, dirs,
                     o_ref, nq_ref, b_ref, d_ref, r, c, eye, n_chunks)
        return carry

    lax.fori_loop(0, n_chunks // GDN_GROUP, prepare, 0)

    def step(t, carry):
        s_f, s_b = carry
        s_f = _gdn_step(t, 0, s_f, o_ref, nq_ref, b_ref, d_ref, n_chunks)
        s_b = _gdn_step(n_chunks - 1 - t, 1, s_b, o_ref, nq_ref, b_ref, d_ref, n_chunks)
        return s_f, s_b

    zero = jnp.zeros((HEAD_DIM, HEAD_DIM), F32)
    lax.fori_loop(0, n_chunks, step, (zero, zero))
    y_ref[...] = _gated_head_norm(o_ref[...], z_ref[...], ng_ref[...]).astype(y_ref.dtype)


def _gdn(p, sm, conv_w, norm_g, batch, seq):
    def col(base):
        return pl.BlockSpec((seq, LANES), lambda b, h: (b, base + h))

    def conv_col(base):
        return pl.BlockSpec((3, LANES), lambda b, h: (0, base + h))

    slots = 2 * (seq // CHUNK)
    return pl.pallas_call(
        _gdn_kernel,
        grid=(batch, HEADS),
        in_specs=[col(_CB_GDN_Q), col(_CB_GDN_K), col(_CB_GDN_V), col(_CB_GDN_Z),
                  pl.BlockSpec((seq, LANES), lambda b, h: (b, 0)),
                  conv_col(0), conv_col(HEADS), conv_col(2 * HEADS),
                  pl.BlockSpec((1, LANES), lambda b, h: (0, 0))],
        out_specs=pl.BlockSpec((seq, LANES), lambda b, h: (b, h)),
        out_shape=jax.ShapeDtypeStruct((batch * seq, HEAD_W), BF16),
        scratch_shapes=[pltpu.VMEM((seq, LANES), F32)] * 8 + [
            pltpu.VMEM((slots * NQ_ROWS, LANES), BF16),
            pltpu.VMEM((slots * HEAD_DIM, LANES), F32),
            pltpu.VMEM((slots, LANES), F32)],
        compiler_params=pltpu.CompilerParams(
            dimension_semantics=("parallel", "arbitrary"), vmem_limit_bytes=VMEM_LIMIT),
        name="gdn",
    )(p, p, p, p, sm, conv_w, conv_w, conv_w, norm_g)


def _merge_out_kernel(yg_ref, yd_ref, gg_ref, gd_ref, x_ref, wbg_ref, wbd_ref, wo_ref, n2_ref,
                      x1_ref, h2_ref):
    merged = (_sigmoid(gg_ref[...]) * _dot(yg_ref[...], wbg_ref[...])
              + _sigmoid(gd_ref[...]) * _dot(yd_ref[...], wbd_ref[...]))
    x1 = x_ref[...] + _dot(merged.astype(BF16), wo_ref[...])
    x1_ref[...] = x1
    h2_ref[...] = (_rms_scale(x1) * n2_ref[...]).astype(h2_ref.dtype)


def _merge_out(y_gla, y_gdn, p, x2, w_bg, w_bd, w_o, n2, tm=256):
    m, d = x2.shape
    gate_cb = (8 * HEAD_W) // d

    def resident(shape):
        return pl.BlockSpec(shape, lambda i: (0, 0), pipeline_mode=pl.Buffered(1))

    return pl.pallas_call(
        _merge_out_kernel,
        grid=(m // tm,),
        in_specs=[
            pl.BlockSpec((tm, HEAD_W), lambda i: (i, 0)),
            pl.BlockSpec((tm, HEAD_W), lambda i: (i, 0)),
            pl.BlockSpec((tm, d), lambda i: (i, gate_cb)),
            pl.BlockSpec((tm, d), lambda i: (i, gate_cb + 1)),
            pl.BlockSpec((tm, d), lambda i: (i, 0)),
            resident((HEAD_W, d)), resident((HEAD_W, d)), resident((d, d)), resident((1, d)),
        ],
        out_specs=[pl.BlockSpec((tm, d), lambda i: (i, 0)), pl.BlockSpec((tm, d), lambda i: (i, 0))],
        out_shape=[jax.ShapeDtypeStruct((m, d), F32), jax.ShapeDtypeStruct((m, d), BF16)],
        compiler_params=pltpu.CompilerParams(
            dimension_semantics=("parallel",), vmem_limit_bytes=VMEM_LIMIT),
        name="merge_out",
    )(y_gla, y_gdn, p, p, x2, w_bg, w_bd, w_o, n2)


def _ffn_up_kernel(h_ref, wg_ref, wv_ref, cg_ref, cv_ref, bg_ref, bv_ref, act_ref):
    h = h_ref[...]
    seq = h.shape[0]
    pos = lax.broadcasted_iota(jnp.int32, (seq, 1), 0)

    def branch(w_ref, c_ref, b_ref):
        u = _dot(h, w_ref[...])
        cw = c_ref[...]
        u_prev = jnp.where(pos == 0, 0.0, pltpu.roll(u, 1, 0))
        u_next = jnp.where(pos == seq - 1, 0.0, pltpu.roll(u, seq - 1, 0))
        return u_prev * cw[0:1, :] + u * cw[1:2, :] + u_next * cw[2:3, :] + b_ref[...]

    gate = branch(wg_ref, cg_ref, bg_ref)
    val = branch(wv_ref, cv_ref, bv_ref)
    act_ref[...] = (_silu(gate) * val).astype(act_ref.dtype)


def _ffn_up(h2, w_up, conv_w, conv_b, batch, seq, tn=256):
    d = h2.shape[1]
    nj = D_FF // tn
    return pl.pallas_call(
        _ffn_up_kernel,
        grid=(batch, nj),
        in_specs=[
            pl.BlockSpec((seq, d), lambda b, j: (b, 0)),
            pl.BlockSpec((d, tn), lambda b, j: (0, j)),
            pl.BlockSpec((d, tn), lambda b, j: (0, nj + j)),
            pl.BlockSpec((3, tn), lambda b, j: (0, j)),
            pl.BlockSpec((3, tn), lambda b, j: (0, nj + j)),
            pl.BlockSpec((1, tn), lambda b, j: (0, j)),
            pl.BlockSpec((1, tn), lambda b, j: (0, nj + j)),
        ],
        out_specs=pl.BlockSpec((seq, tn), lambda b, j: (b, j)),
        out_shape=jax.ShapeDtypeStruct((batch * seq, D_FF), BF16),
        compiler_params=pltpu.CompilerParams(
            dimension_semantics=("parallel", "arbitrary"), vmem_limit_bytes=VMEM_LIMIT),
        name="ffn_up",
    )(h2, w_up, w_up, conv_w, conv_w, conv_b, conv_b)


def _ffn_down_kernel(act_ref, w_ref, x1_ref, g_ref, out_ref, acc_ref, *, final_norm):
    kk = pl.program_id(1)

    @pl.when(kk == 0)
    def _():
        acc_ref[...] = x1_ref[...]

    acc_ref[...] += _dot(act_ref[...], w_ref[...])

    @pl.when(kk == pl.num_programs(1) - 1)
    def _():
        x2 = acc_ref[...]
        out_ref[...] = _rms_scale(x2) * g_ref[...] if final_norm else x2


def _ffn_down(act, w_down, x1, g, final_norm, tm=512, tk=1408):
    m, d = x1.shape
    return pl.pallas_call(
        functools.partial(_ffn_down_kernel, final_norm=final_norm),
        grid=(m // tm, D_FF // tk),
        in_specs=[
            pl.BlockSpec((tm, tk), lambda i, k: (i, k)),
            pl.BlockSpec((tk, d), lambda i, k: (k, 0)),
            pl.BlockSpec((tm, d), lambda i, k: (i, 0)),
            pl.BlockSpec((1, d), lambda i, k: (0, 0)),
        ],
        out_specs=pl.BlockSpec((tm, d), lambda i, k: (i, 0)),
        out_shape=jax.ShapeDtypeStruct((m, d), F32),
        scratch_shapes=[pltpu.VMEM((tm, d), F32)],
        compiler_params=pltpu.CompilerParams(
            dimension_semantics=("parallel", "arbitrary"), vmem_limit_bytes=VMEM_LIMIT),
        name="ffn_down",
    )(act, w_down, x1, g)


def _pad_rows(w, row0, rows):
    return jnp.zeros((rows, w.shape[1]), w.dtype).at[row0:row0 + w.shape[0]].set(w)


def kernel(x, norm1_g, w_in, gla_decay_w_f, gla_decay_b_f, gla_decay_w_b, gla_decay_b_b, gla_norm_g,
           gdn_conv_w, gdn_a_log_f, gdn_dt_bias_f, gdn_a_log_b, gdn_dt_bias_b, gdn_norm_g,
           w_branch_gla, w_branch_gdn, w_out, norm2_g, w_up, ffn_conv_w, ffn_conv_b, w_down,
           final_norm_g):
    batch, seq, d = x.shape
    depth = w_in.shape[0]
    x2 = x.reshape(batch * seq, d)
    for l in range(depth):
        wl = w_in[l]
        w_big = jnp.concatenate(
            [wl[:, :_OFF_LR], wl[:, _OFF_GDN:_OFF_AB], wl[:, _OFF_GATES:]], axis=1).astype(BF16)
        w_small = jnp.concatenate(
            [wl[:, _OFF_LR:_OFF_GDN], wl[:, _OFF_AB:_OFF_GATES],
             jnp.zeros((d, LANES - 2 * GLA_LOW_RANK - 4 * HEADS), wl.dtype)], axis=1).astype(BF16)
        pad = jnp.zeros((LANES - _SM_BETA_F,), F32)

        def gate_row(fwd, bwd):
            return jnp.concatenate([jnp.zeros((_SM_A_F,), F32), fwd, bwd, pad])[None, :]

        p, sm = _in_proj(x2, norm1_g[l][None, :], w_big, w_small,
                         gate_row(gdn_a_log_f[l], gdn_a_log_b[l]),
                         gate_row(gdn_dt_bias_f[l], gdn_dt_bias_b[l]))

        y_gla = _gla(p, sm,
                     _pad_rows(gla_decay_w_f[l], 0, LANES).astype(BF16), gla_decay_b_f[l][None, :],
                     _pad_rows(gla_decay_w_b[l], GLA_LOW_RANK, LANES).astype(BF16), gla_decay_b_b[l][None, :],
                     gla_norm_g[l][None, :], batch, seq)
        y_gdn = _gdn(p, sm, gdn_conv_w[l], gdn_norm_g[l][None, :], batch, seq)

        x2, h2 = _merge_out(y_gla, y_gdn, p, x2,
                            w_branch_gla[l].astype(BF16), w_branch_gdn[l].astype(BF16),
                            w_out[l].astype(BF16), norm2_g[l][None, :])
        act = _ffn_up(h2, w_up[l].astype(BF16), ffn_conv_w[l], ffn_conv_b[l][None, :], batch, seq)
        x2 = _ffn_down(act, w_down[l].astype(BF16), x2, final_norm_g[None, :], l == depth - 1)
    return x2.reshape(batch, seq, d)
```

```python
import functools

import jax
import jax.numpy as jnp
from jax import lax
from jax.experimental import pallas as pl
from jax.experimental.pallas import tpu as pltpu

F32 = jnp.float32
BF16 = jnp.bfloat16

D_MODEL = 2048
HEADS = 8
HEAD_DIM = 128
HEAD_W = HEADS * HEAD_DIM
GLA_LOW_RANK = 16
GLA_GATE_NORMALIZER = 16.0
D_FF = 5632
CHUNK = 64
NORM_EPS = 1e-6
LANES = 128
SUBLANES = 8
CUM_ROWS = 256
FFN_ROWS = 256
GLA_UNROLL = 4
GDN_GROUP = 8
NQ_ROWS = HEAD_DIM + CHUNK

_OFF_LR = 4 * HEAD_W
_OFF_GDN = _OFF_LR + 2 * GLA_LOW_RANK
_OFF_AB = _OFF_GDN + 4 * HEAD_W
_OFF_GATES = _OFF_AB + 4 * HEADS
N_GROUP = 4 * HEAD_W
assert _OFF_LR == N_GROUP and _OFF_AB - _OFF_GDN == N_GROUP and 2 * D_MODEL == N_GROUP
_SM_A_F, _SM_A_B, _SM_BETA_F, _SM_BETA_B = 32, 40, 48, 56
_CB_GLA_Q, _CB_GLA_K, _CB_GLA_V, _CB_GLA_GATE = 0, 8, 16, 24
_CB_GDN_Q, _CB_GDN_K, _CB_GDN_V, _CB_GDN_Z = 32, 40, 48, 56

VMEM_LIMIT = 56 * 1024 * 1024


def _dot(a, b):
    return jnp.dot(a, b, preferred_element_type=F32)


def _dot_nt(a, b):
    return lax.dot_general(a, b, (((1,), (1,)), ((), ())), preferred_element_type=F32)


def _dot_tn(a, b):
    return lax.dot_general(a, b, (((0,), (0,)), ((), ())), preferred_element_type=F32)


def _split(x):
    hi = x.astype(BF16)
    lo = (x - hi.astype(F32)).astype(BF16)
    return hi, lo


def _dot_exact_rhs(a_bf, b):
    hi, lo = _split(b)
    return _dot(a_bf, hi) + _dot(a_bf, lo)


def _dot_exact_lhs(a, b_bf):
    hi, lo = _split(a)
    return _dot(hi, b_bf) + _dot(lo, b_bf)


def _dot3(a, b):
    ah, al = _split(a)
    bh, bl = _split(b)
    return _dot(ah, bh) + (_dot(ah, bl) + _dot(al, bh))


def _sigmoid(x):
    return 0.5 * jnp.tanh(0.5 * x) + 0.5


def _silu_of_half(h):
    return h * jnp.tanh(h) + h


def _silu(x):
    return _silu_of_half(0.5 * x)


def _softplus(x):
    return jnp.maximum(x, 0.0) + jnp.log1p(jnp.exp(-jnp.abs(x)))


def _conv3(x, cw, row_before=None, row_after=None):
    n = x.shape[0]
    sub = lax.broadcasted_iota(jnp.int32, (SUBLANES, 1), 0)
    zero = jnp.zeros((1, x.shape[1]), x.dtype)
    x_prev = pltpu.roll(x, 1, 0)
    first = jnp.where(sub == 0, zero if row_before is None else row_before, x_prev[:SUBLANES, :])
    x_prev = jnp.concatenate([first, x_prev[SUBLANES:, :]], axis=0)
    x_next = pltpu.roll(x, n - 1, 0)
    last = jnp.where(sub == SUBLANES - 1, zero if row_after is None else row_after, x_next[n - SUBLANES:, :])
    x_next = jnp.concatenate([x_next[:n - SUBLANES, :], last], axis=0)
    return x_prev * cw[0:1, :] + x * cw[1:2, :] + x_next * cw[2:3, :]


def _rms_scale(x):
    return x * lax.rsqrt(jnp.mean(x * x, axis=-1, keepdims=True) + NORM_EPS)


def _repack_kernel(w_ref, o_ref, *, offsets):
    width = o_ref.shape[2]
    for g, off in enumerate(offsets):
        o_ref[g] = w_ref[:, off:off + width].astype(o_ref.dtype)


def _repack_w_in(w, offsets, width, tr=256):
    d, n_in = w.shape
    return pl.pallas_call(
        functools.partial(_repack_kernel, offsets=offsets),
        grid=(d // tr,),
        in_specs=[pl.BlockSpec((tr, n_in), lambda i: (i, 0))],
        out_specs=pl.BlockSpec((len(offsets), tr, width), lambda i: (0, i, 0)),
        out_shape=jax.ShapeDtypeStruct((len(offsets), d, width), BF16),
        compiler_params=pltpu.CompilerParams(
            dimension_semantics=("parallel",), vmem_limit_bytes=VMEM_LIMIT),
        name="repack_w_in",
    )(w)


def _gdn_gates(sm, alog_row, dtb_row):
    tm = sm.shape[0]
    col = lax.broadcasted_iota(jnp.int32, (1, LANES), 1)
    is_g = (col >= _SM_A_F) & (col < _SM_BETA_F)
    is_fwd = col < _SM_A_B
    is_beta = (col >= _SM_BETA_F) & (col < _SM_BETA_B + HEADS)
    g = jnp.where(is_g, -jnp.exp(alog_row) * _softplus(sm + dtb_row), 0.0)
    r = lax.broadcasted_iota(jnp.int32, (CUM_ROWS, CUM_ROWS), 0)
    c = lax.broadcasted_iota(jnp.int32, (CUM_ROWS, CUM_ROWS), 1)
    same_chunk = (r // CHUNK) == (c // CHUNK)
    tril_bf = (same_chunk & (r >= c)).astype(F32).astype(BF16)
    triu_bf = (same_chunk & (r <= c)).astype(F32).astype(BF16)
    slabs = []
    for s in range(tm // CUM_ROWS):
        hi, lo = _split(g[s * CUM_ROWS:(s + 1) * CUM_ROWS])
        prefix = _dot(tril_bf, hi) + _dot(tril_bf, lo)
        suffix = _dot(triu_bf, hi) + _dot(triu_bf, lo)
        slabs.append(jnp.where(is_fwd, prefix, suffix))
    g_cum = jnp.concatenate(slabs, axis=0)
    return jnp.where(is_g, g_cum, jnp.where(is_beta, _sigmoid(sm), sm))


def _in_proj_kernel(x_ref, g_ref, w_ref, ws_ref, alog_ref, dtb_ref, p_ref, sm_ref, hn_ref):
    @pl.when(pl.program_id(1) == 0)
    def _():
        hn = (_rms_scale(x_ref[...]) * g_ref[...]).astype(BF16)
        hn_ref[...] = hn
        sm_ref[...] = _gdn_gates(_dot(hn, ws_ref[...]), alog_ref[...], dtb_ref[...])

    p_ref[...] = _dot(hn_ref[...], w_ref[...])


def _in_proj(x2, g, w_big, w_small, alog_row, dtb_row, tm=1024, tn=1024):
    m, d = x2.shape
    groups, _, n_group = w_big.shape
    per_group = n_group // tn
    n = groups * n_group
    return pl.pallas_call(
        _in_proj_kernel,
        grid=(m // tm, n // tn),
        in_specs=[
            pl.BlockSpec((tm, d), lambda i, j: (i, 0)),
            pl.BlockSpec((1, d), lambda i, j: (0, 0)),
            pl.BlockSpec((None, d, tn), lambda i, j: (j // per_group, 0, j % per_group)),
            pl.BlockSpec((d, LANES), lambda i, j: (0, 0)),
            pl.BlockSpec((1, LANES), lambda i, j: (0, 0)),
            pl.BlockSpec((1, LANES), lambda i, j: (0, 0)),
        ],
        out_specs=[
            pl.BlockSpec((tm, tn), lambda i, j: (i, j)),
            pl.BlockSpec((tm, LANES), lambda i, j: (i, 0)),
        ],
        out_shape=[
            jax.ShapeDtypeStruct((m, n), F32),
            jax.ShapeDtypeStruct((m, LANES), F32),
        ],
        scratch_shapes=[pltpu.VMEM((tm, d), BF16)],
        compiler_params=pltpu.CompilerParams(
            dimension_semantics=("parallel", "arbitrary"), vmem_limit_bytes=VMEM_LIMIT),
        name="in_proj",
    )(x2, g, w_big, w_small, alog_row, dtb_row)


def _chunk_masks():
    r = lax.broadcasted_iota(jnp.int32, (CHUNK, CHUNK), 0)
    c = lax.broadcasted_iota(jnp.int32, (CHUNK, CHUNK), 1)
    return r, c


def _gated_head_norm(o, z, g):
    return (_rms_scale(o) * g) * _silu(z)


def _gla_chunks(chunks, states, q_ref, k_ref, v_ref, dirs, o_ref):
    pairs = []
    for u in range(len(chunks[0])):
        for d, (lg_ref, incl, tri_bf, edge) in enumerate(dirs):
            rows = pl.ds(pl.multiple_of(chunks[d][u] * CHUNK, CHUNK), CHUNK)
            hi, lo = _split(lg_ref[rows, :])
            pairs.append(dict(d=d, rows=rows, incl=incl, edge=edge, hi=hi, lo=lo, tri=tri_bf))
    for p in pairs:
        p["g_cum"] = _dot(p["tri"], p["hi"]) + _dot(p["tri"], p["lo"])
    for p in pairs:
        g_cum = p["g_cum"]
        g_edge = g_cum[p["edge"]:p["edge"] + 1, :]
        k = k_ref[p["rows"], :]
        p["v_bf"] = _bf(v_ref[p["rows"], :])
        p["q_dec"] = _bf(q_ref[p["rows"], :] * (HEAD_DIM ** -0.5) * jnp.exp(g_cum))
        p["decay"] = jnp.exp(g_edge)
        p["scores"] = _dot_nt(p["q_dec"], _bf(k * jnp.exp(-g_cum)))
        p["upd"] = _dot_tn(p["v_bf"], _bf(k * jnp.exp(g_edge - g_cum)))
    for p in pairs:
        p["o"] = _dot(_bf(jnp.where(p["incl"], p["scores"], 0.0)), p["v_bf"])
    states = list(states)
    for p in pairs:
        st = states[p["d"]]
        o_ref[p["rows"], :] += p["o"] + _dot_nt(p["q_dec"], _bf(st))
        states[p["d"]] = st * p["decay"] + p["upd"]
    return tuple(states)


def _gla_kernel(q_ref, k_ref, v_ref, gate_ref, sm_ref, dwf_ref, dbf_ref, dwb_ref, dbb_ref, ng_ref,
                y_ref, lgf_ref, lgb_ref, o_ref):
    seq = q_ref.shape[0]
    n_chunks = seq // CHUNK
    sm_bf = sm_ref[...].astype(BF16)

    def log_decay(dw_ref, db_ref):
        pre = _dot(sm_bf, dw_ref[...]) + db_ref[...]
        return -_softplus(-pre) / GLA_GATE_NORMALIZER

    lgf_ref[...] = log_decay(dwf_ref, dbf_ref)
    lgb_ref[...] = log_decay(dwb_ref, dbb_ref)
    o_ref[...] = jnp.zeros_like(o_ref)

    r, c = _chunk_masks()
    lower, upper = r >= c, r <= c
    tril_bf = lower.astype(F32).astype(BF16)
    triu_bf = upper.astype(F32).astype(BF16)

    dirs = ((lgf_ref, lower, tril_bf, CHUNK - 1), (lgb_ref, upper, triu_bf, 0))

    def body(t, carry):
        fwd = [t * GLA_UNROLL + u for u in range(GLA_UNROLL)]
        bwd = [n_chunks - 1 - n for n in fwd]
        return _gla_chunks((fwd, bwd), carry, q_ref, k_ref, v_ref, dirs, o_ref)

    zero = jnp.zeros((HEAD_DIM, HEAD_DIM), F32)
    lax.fori_loop(0, n_chunks // GLA_UNROLL, body, (zero, zero))
    y_ref[...] = _gated_head_norm(o_ref[...], gate_ref[...], ng_ref[...]).astype(y_ref.dtype)


def _gla(p, sm, dw_f, db_f, dw_b, db_b, norm_g, batch, seq):
    def col(base):
        return pl.BlockSpec((seq, LANES), lambda b, h: (b, base + h))

    head_vec = pl.BlockSpec((1, LANES), lambda b, h: (0, h))
    head_mat = pl.BlockSpec((LANES, LANES), lambda b, h: (0, h))
    shared_vec = pl.BlockSpec((1, LANES), lambda b, h: (0, 0))
    return pl.pallas_call(
        _gla_kernel,
        grid=(batch, HEADS),
        in_specs=[col(_CB_GLA_Q), col(_CB_GLA_K), col(_CB_GLA_V), col(_CB_GLA_GATE),
                  pl.BlockSpec((seq, LANES), lambda b, h: (b, 0)),
                  head_mat, head_vec, head_mat, head_vec, shared_vec],
        out_specs=pl.BlockSpec((seq, LANES), lambda b, h: (b, h)),
        out_shape=jax.ShapeDtypeStruct((batch * seq, HEAD_W), BF16),
        scratch_shapes=[pltpu.VMEM((seq, LANES), F32)] * 3,
        compiler_params=pltpu.CompilerParams(
            dimension_semantics=("parallel", "arbitrary"), vmem_limit_bytes=VMEM_LIMIT),
        name="gla",
    )(p, p, p, p, sm, dw_f, db_f, dw_b, db_b, norm_g)


def _bf(x):
    return x.astype(BF16)


def _unit_tri_inverses(l_mats, r, c, eye):
    blk16 = (r // 16) == (c // 16)
    blk32 = (r // 32) == (c // 32)
    a = [-jnp.where(blk16, l, 0.0) for l in l_mats]
    a_bf = [_bf(x) for x in a]
    p2_bf = [_bf(_dot(x, x)) for x in a_bf]
    yield
    t = [eye + x for x in a]
    t = [x + _dot(_bf(x), p) for x, p in zip(t, p2_bf)]
    p4_bf = [_bf(_dot(p, p)) for p in p2_bf]
    yield
    t = [x + _dot(_bf(x), p) for x, p in zip(t, p4_bf)]
    p8_bf = [_bf(_dot(p, p)) for p in p4_bf]
    yield
    t = [x + _dot(_bf(x), p) for x, p in zip(t, p8_bf)]
    yield
    for off_diag in (blk32 & ~blk16, ~blk32):
        t_bf = [_bf(x) for x in t]
        e_t = [_bf(_dot(_bf(jnp.where(off_diag, l, 0.0)), x)) for l, x in zip(l_mats, t_bf)]
        yield
        t = [x - _dot(xb, e) for x, xb, e in zip(t, t_bf, e_t)]
        yield
    return t


def _interleave(stages, actions):
    actions = list(actions)
    for _ in stages:
        if actions:
            actions.pop(0)()
    for act in actions:
        act()


def _gdn_prepare(chunks, qs_ref, ks_ref, vs_ref, dirs, o_ref, nq_ref, b_ref, d_ref, r, c, eye, n_chunks):
    seq = n_chunks * CHUNK
    pairs = []
    for u in range(len(chunks[0])):
        for d, (g_ref, beta_ref, incl, strict, edge) in enumerate(dirs):
            n = chunks[d][u]
            row0 = pl.multiple_of(n * CHUNK, CHUNK)
            rows = pl.ds(row0, CHUNK)
            pairs.append(dict(n=n, d=d, row0=row0, q=qs_ref[rows, :], k=ks_ref[rows, :], v=vs_ref[rows, :],
                              g_cum=g_ref[rows, :], beta=beta_ref[rows, :],
                              incl=incl, strict=strict, edge=edge))
    for p in pairs:
        k_bf = _bf(p["k"])
        p["kk"] = _dot_nt(k_bf, k_bf)
        p["qk"] = _dot_nt(_bf(p["q"]), k_bf)
    yield
    for p in pairs:
        g_cum, beta = p["g_cum"], p["beta"]
        g_cum_t = g_cum.T
        diff = g_cum[:, :CHUNK] - g_cum_t[:CHUNK, :]
        decay = jnp.where(p["incl"], jnp.exp(jnp.where(p["incl"], diff, 0.0)), 0.0)
        p["l_mat"] = jnp.where(p["strict"], p["kk"], 0.0) * decay * beta[:, :CHUNK]
        p["e_g"] = jnp.exp(g_cum)
        p["g_edge"] = g_cum[p["edge"]:p["edge"] + 1, :]
        p["rhs"] = _bf(jnp.concatenate([p["k"] * (beta * p["e_g"]), p["v"] * beta], axis=1))
        p["attn"] = _bf(p["qk"] * decay)
        p["k_tail_t"] = _bf(p["k"].T * jnp.exp(p["g_edge"][:, :CHUNK] - g_cum_t))
    t_inv = yield from _unit_tri_inverses([p["l_mat"] for p in pairs], r, c, eye)
    wu_bf = [_bf(_dot(_bf(t), p["rhs"])) for t, p in zip(t_inv, pairs)]
    yield
    aw = [_dot(p["attn"], x) for p, x in zip(pairs, wu_bf)]
    kw = [_dot(p["k_tail_t"], x) for p, x in zip(pairs, wu_bf)]
    for p, aw_p, kw_p in zip(pairs, aw, kw):
        slot = p["d"] * n_chunks + p["n"]
        nq_rows = pl.multiple_of(slot * NQ_ROWS, NQ_ROWS)
        nq_ref[pl.ds(nq_rows, HEAD_DIM), :] = _bf(-kw_p[:, :LANES])
        nq_ref[pl.ds(nq_rows + HEAD_DIM, CHUNK), :] = _bf(p["q"] * p["e_g"] - aw_p[:, :LANES])
        b_ref[pl.ds(pl.multiple_of(slot * HEAD_DIM, HEAD_DIM), HEAD_DIM), :] = kw_p[:, LANES:]
        d_ref[pl.ds(slot, 1), :] = jnp.exp(p["g_edge"])
        o_ref[pl.ds(p["d"] * seq + p["row0"], CHUNK), :] = aw_p[:, LANES:]


def _gdn_step_actions(chunks, states, o_ref, nq_ref, b_ref, d_ref, n_chunks):
    seq = n_chunks * CHUNK

    def step(u):
        for d in range(2):
            n = chunks[d][u]
            slot = d * n_chunks + n
            nq = nq_ref[pl.ds(pl.multiple_of(slot * NQ_ROWS, NQ_ROWS), NQ_ROWS), :]
            res = _dot(nq, _bf(states[d]))
            o_ref[pl.ds((2 + d) * seq + pl.multiple_of(n * CHUNK, CHUNK), CHUNK), :] = res[HEAD_DIM:, :]
            b_mat = b_ref[pl.ds(pl.multiple_of(slot * HEAD_DIM, HEAD_DIM), HEAD_DIM), :]
            states[d] = states[d] * d_ref[pl.ds(slot, 1), :] + (res[:HEAD_DIM, :] + b_mat)

    return [functools.partial(step, u) for u in range(len(chunks[0]))]


def _gdn_kernel(q_ref, k_ref, v_ref, z_ref, sm_ref, cq_ref, ck_ref, cv_ref, ng_ref,
                y_ref, qs_ref, ks_ref, vs_ref, gf_ref, gb_ref, bf_ref, bb_ref, o_ref, nq_ref, b_ref, d_ref):
    h = pl.program_id(1)
    seq = q_ref.shape[0]
    n_chunks = seq // CHUNK

    n_groups = n_chunks // GDN_GROUP
    slab_rows = GDN_GROUP * CHUNK
    sel_row = lax.broadcasted_iota(jnp.int32, (LANES, 4 * LANES), 0)
    sel_grp = lax.broadcasted_iota(jnp.int32, (LANES, 4 * LANES), 1) // LANES
    sel = (sel_row == _SM_A_F + HEADS * sel_grp + h).astype(F32).astype(BF16)

    def l2_norm(t, scale=1.0):
        return t * (lax.rsqrt(jnp.sum(t * t, axis=-1, keepdims=True) + NORM_EPS) * scale)

    def load_slab(s):
        r0, r1 = s * slab_rows, (s + 1) * slab_rows

        def conv_silu(x_ref, c_ref):
            before = x_ref[r0 - 1:r0, :] if s > 0 else None
            after = x_ref[r1:r1 + 1, :] if r1 < seq else None
            return _silu_of_half(_conv3(x_ref[r0:r1, :], 0.5 * c_ref[...], before, after))

        qs_ref[r0:r1, :] = l2_norm(conv_silu(q_ref, cq_ref), HEAD_DIM ** -0.5)
        ks_ref[r0:r1, :] = l2_norm(conv_silu(k_ref, ck_ref))
        vs_ref[r0:r1, :] = conv_silu(v_ref, cv_ref)
        sm_hi, sm_lo = _split(sm_ref[r0:r1, :])
        cols = _dot(sm_hi, sel) + _dot(sm_lo, sel)
        for i, ref in enumerate((gf_ref, gb_ref, bf_ref, bb_ref)):
            ref[r0:r1, :] = cols[:, i * LANES:(i + 1) * LANES]

    r, c = _chunk_masks()
    eye = (r == c).astype(F32)
    dirs = ((gf_ref, bf_ref, r >= c, r > c, CHUNK - 1),
            (gb_ref, bb_ref, r <= c, r < c, 0))

    def group(t):
        fwd = [t * GDN_GROUP + u for u in range(GDN_GROUP)]
        return fwd, [n_chunks - 1 - n for n in fwd]

    def prepare(t):
        return _gdn_prepare(group(t), qs_ref, ks_ref, vs_ref, dirs, o_ref, nq_ref, b_ref, d_ref,
                            r, c, eye, n_chunks)

    def advance(t, states):
        return _gdn_step_actions(group(t), states, o_ref, nq_ref, b_ref, d_ref, n_chunks)

    def body(t, states):
        states = list(states)
        _interleave(prepare(t), advance(t - 1, states))
        return tuple(states)

    zero = jnp.zeros((HEAD_DIM, HEAD_DIM), F32)
    load_slab(0)
    load_slab(n_groups - 1)
    _interleave(prepare(0), [functools.partial(load_slab, s) for s in range(1, n_groups - 1)])
    states = list(lax.fori_loop(1, n_groups, body, (zero, zero)))
    _interleave(iter(()), advance(n_groups - 1, states))
    o = o_ref[0:seq, :] + o_ref[seq:2 * seq, :] + o_ref[2 * seq:3 * seq, :] + o_ref[3 * seq:4 * seq, :]
    y_ref[...] = _gated_head_norm(o, z_ref[...], ng_ref[...]).astype(y_ref.dtype)


def _gdn(p, sm, conv_w, norm_g, batch, seq):
    def col(base):
        return pl.BlockSpec((seq, LANES), lambda b, h: (b, base + h))

    def conv_col(base):
        return pl.BlockSpec((3, LANES), lambda b, h: (0, base + h))

    slots = 2 * (seq // CHUNK)
    return pl.pallas_call(
        _gdn_kernel,
        grid=(batch, HEADS),
        in_specs=[col(_CB_GDN_Q), col(_CB_GDN_K), col(_CB_GDN_V), col(_CB_GDN_Z),
                  pl.BlockSpec((seq, LANES), lambda b, h: (b, 0)),
                  conv_col(0), conv_col(HEADS), conv_col(2 * HEADS),
                  pl.BlockSpec((1, LANES), lambda b, h: (0, 0))],
        out_specs=pl.BlockSpec((seq, LANES), lambda b, h: (b, h)),
        out_shape=jax.ShapeDtypeStruct((batch * seq, HEAD_W), BF16),
        scratch_shapes=[pltpu.VMEM((seq, LANES), F32)] * 7 + [
            pltpu.VMEM((4 * seq, LANES), F32),
            pltpu.VMEM((slots * NQ_ROWS, LANES), BF16),
            pltpu.VMEM((slots * HEAD_DIM, LANES), F32),
            pltpu.VMEM((slots, LANES), F32)],
        compiler_params=pltpu.CompilerParams(
            dimension_semantics=("parallel", "arbitrary"), vmem_limit_bytes=VMEM_LIMIT),
        name="gdn",
    )(p, p, p, p, sm, conv_w, conv_w, conv_w, norm_g)


def _merge_out_kernel(yg_ref, yd_ref, gg_ref, gd_ref, x_ref, wbg_ref, wbd_ref, wo_ref, n2_ref,
                      x1_ref, h2_ref):
    merged = (_sigmoid(gg_ref[...]) * _dot(yg_ref[...], wbg_ref[...])
              + _sigmoid(gd_ref[...]) * _dot(yd_ref[...], wbd_ref[...]))
    x1 = x_ref[...] + _dot(merged.astype(BF16), wo_ref[...])
    x1_ref[...] = x1
    h2_ref[...] = (_rms_scale(x1) * n2_ref[...]).astype(h2_ref.dtype)


def _merge_out(y_gla, y_gdn, p, x2, w_bg, w_bd, w_o, n2, tm=256):
    m, d = x2.shape
    gate_cb = (8 * HEAD_W) // d

    def resident(shape):
        return pl.BlockSpec(shape, lambda i: (0, 0), pipeline_mode=pl.Buffered(1))

    return pl.pallas_call(
        _merge_out_kernel,
        grid=(m // tm,),
        in_specs=[
            pl.BlockSpec((tm, HEAD_W), lambda i: (i, 0)),
            pl.BlockSpec((tm, HEAD_W), lambda i: (i, 0)),
            pl.BlockSpec((tm, d), lambda i: (i, gate_cb)),
            pl.BlockSpec((tm, d), lambda i: (i, gate_cb + 1)),
            pl.BlockSpec((tm, d), lambda i: (i, 0)),
            resident((HEAD_W, d)), resident((HEAD_W, d)), resident((d, d)), resident((1, d)),
        ],
        out_specs=[pl.BlockSpec((tm, d), lambda i: (i, 0)), pl.BlockSpec((tm, d), lambda i: (i, 0))],
        out_shape=[jax.ShapeDtypeStruct((m, d), F32), jax.ShapeDtypeStruct((m, d), BF16)],
        compiler_params=pltpu.CompilerParams(
            dimension_semantics=("parallel",), vmem_limit_bytes=VMEM_LIMIT),
        name="merge_out",
    )(y_gla, y_gdn, p, p, x2, w_bg, w_bd, w_o, n2)


def _ffn_up_kernel(h_ref, wg_ref, wv_ref, cg_ref, cv_ref, bg_ref, bv_ref, act_ref, u_ref):
    seq = h_ref.shape[0]
    n_slabs = seq // FFN_ROWS
    lo, hi = SUBLANES, SUBLANES + FFN_ROWS
    params = ((wg_ref, 0.5 * cg_ref[...], 0.5 * bg_ref[...]), (wv_ref, cv_ref[...], bv_ref[...]))
    zero_row = jnp.zeros((1, act_ref.shape[1]), F32)

    def project(s):
        h = h_ref[s * FFN_ROWS:(s + 1) * FFN_ROWS, :]
        edges = []
        for i, (w_ref, _, _) in enumerate(params):
            u = _dot(h, w_ref[...])
            u_ref[s % 2, i, lo:hi, :] = u
            edges.append((u[:1, :], u[FFN_ROWS - 1:, :]))
        return edges

    def finish(s, before, after):
        out = []
        for i, (_, cw, b) in enumerate(params):
            u_ref[s % 2, i, lo - 1:lo, :] = zero_row if before is None else before[i][1]
            u_ref[s % 2, i, hi:hi + 1, :] = zero_row if after is None else after[i][0]
            out.append(u_ref[s % 2, i, lo - 1:hi - 1, :] * cw[0:1, :] + u_ref[s % 2, i, lo:hi, :] * cw[1:2, :]
                       + u_ref[s % 2, i, lo + 1:hi + 1, :] * cw[2:3, :] + b)
        act_ref[s * FFN_ROWS:(s + 1) * FFN_ROWS, :] = (_silu_of_half(out[0]) * out[1]).astype(act_ref.dtype)

    edges = [project(0)]
    for s in range(1, n_slabs):
        edges.append(project(s))
        finish(s - 1, edges[s - 2] if s >= 2 else None, edges[s])
    finish(n_slabs - 1, edges[-2] if n_slabs >= 2 else None, None)


def _ffn_up(h2, w_up, conv_w, conv_b, batch, seq, tn=256):
    d = h2.shape[1]
    nj = D_FF // tn
    return pl.pallas_call(
        _ffn_up_kernel,
        grid=(batch, nj),
        in_specs=[
            pl.BlockSpec((seq, d), lambda b, j: (b, 0)),
            pl.BlockSpec((d, tn), lambda b, j: (0, j)),
            pl.BlockSpec((d, tn), lambda b, j: (0, nj + j)),
            pl.BlockSpec((3, tn), lambda b, j: (0, j)),
            pl.BlockSpec((3, tn), lambda b, j: (0, nj + j)),
            pl.BlockSpec((1, tn), lambda b, j: (0, j)),
            pl.BlockSpec((1, tn), lambda b, j: (0, nj + j)),
        ],
        out_specs=pl.BlockSpec((seq, tn), lambda b, j: (b, j)),
        out_shape=jax.ShapeDtypeStruct((batch * seq, D_FF), BF16),
        scratch_shapes=[pltpu.VMEM((2, 2, FFN_ROWS + 2 * SUBLANES, tn), F32)],
        compiler_params=pltpu.CompilerParams(
            dimension_semantics=("parallel", "arbitrary"), vmem_limit_bytes=VMEM_LIMIT),
        name="ffn_up",
    )(h2, w_up, w_up, conv_w, conv_w, conv_b, conv_b)


def _ffn_down_kernel(act_ref, w_ref, x1_ref, g_ref, out_ref, *, final_norm):
    x2 = x1_ref[...] + _dot(act_ref[...], w_ref[...])
    out_ref[...] = _rms_scale(x2) * g_ref[...] if final_norm else x2


def _ffn_down(act, w_down, x1, g, final_norm, tm=256):
    m, d = x1.shape
    return pl.pallas_call(
        functools.partial(_ffn_down_kernel, final_norm=final_norm),
        grid=(m // tm,),
        in_specs=[
            pl.BlockSpec((tm, D_FF), lambda i: (i, 0)),
            pl.BlockSpec((D_FF, d), lambda i: (0, 0), pipeline_mode=pl.Buffered(1)),
            pl.BlockSpec((tm, d), lambda i: (i, 0)),
            pl.BlockSpec((1, d), lambda i: (0, 0)),
        ],
        out_specs=pl.BlockSpec((tm, d), lambda i: (i, 0)),
        out_shape=jax.ShapeDtypeStruct((m, d), F32),
        compiler_params=pltpu.CompilerParams(
            dimension_semantics=("parallel",), vmem_limit_bytes=VMEM_LIMIT),
        name="ffn_down",
    )(act, w_down, x1, g)


def _pad_rows(w, row0, rows):
    return jnp.zeros((rows, w.shape[1]), w.dtype).at[row0:row0 + w.shape[0]].set(w)


def kernel(x, norm1_g, w_in, gla_decay_w_f, gla_decay_b_f, gla_decay_w_b, gla_decay_b_b, gla_norm_g,
           gdn_conv_w, gdn_a_log_f, gdn_dt_bias_f, gdn_a_log_b, gdn_dt_bias_b, gdn_norm_g,
           w_branch_gla, w_branch_gdn, w_out, norm2_g, w_up, ffn_conv_w, ffn_conv_b, w_down,
           final_norm_g):
    batch, seq, d = x.shape
    depth = w_in.shape[0]
    x2 = x.reshape(batch * seq, d)
    for l in range(depth):
        wl = w_in[l]
        w_big = _repack_w_in(wl, (0, _OFF_GDN, _OFF_GATES), N_GROUP)
        w_small = jnp.concatenate(
            [wl[:, _OFF_LR:_OFF_GDN], wl[:, _OFF_AB:_OFF_GATES],
             jnp.zeros((d, LANES - 2 * GLA_LOW_RANK - 4 * HEADS), wl.dtype)], axis=1).astype(BF16)
        pad = jnp.zeros((LANES - _SM_BETA_F,), F32)

        def gate_row(fwd, bwd):
            return jnp.concatenate([jnp.zeros((_SM_A_F,), F32), fwd, bwd, pad])[None, :]

        p, sm = _in_proj(x2, norm1_g[l][None, :], w_big, w_small,
                         gate_row(gdn_a_log_f[l], gdn_a_log_b[l]),
                         gate_row(gdn_dt_bias_f[l], gdn_dt_bias_b[l]))

        y_gla = _gla(p, sm,
                     _pad_rows(gla_decay_w_f[l], 0, LANES).astype(BF16), gla_decay_b_f[l][None, :],
                     _pad_rows(gla_decay_w_b[l], GLA_LOW_RANK, LANES).astype(BF16), gla_decay_b_b[l][None, :],
                     gla_norm_g[l][None, :], batch, seq)
        y_gdn = _gdn(p, sm, gdn_conv_w[l], gdn_norm_g[l][None, :], batch, seq)

        x2, h2 = _merge_out(y_gla, y_gdn, p, x2,
                            w_branch_gla[l].astype(BF16), w_branch_gdn[l].astype(BF16),
                            w_out[l].astype(BF16), norm2_g[l][None, :])
        act = _ffn_up(h2, w_up[l].astype(BF16), ffn_conv_w[l], ffn_conv_b[l][None, :], batch, seq)
        x2 = _ffn_down(act, w_down[l].astype(BF16), x2, final_norm_g[None, :], l == depth - 1)
    return x2.reshape(batch, seq, d)
```

```python
import functools
import math

import jax
import jax.numpy as jnp
from jax import lax
from jax.experimental import pallas as pl
from jax.experimental.pallas import tpu as pltpu

F32 = jnp.float32
BF16 = jnp.bfloat16

D_MODEL = 2048
HEADS = 8
HEAD_DIM = 128
HEAD_W = HEADS * HEAD_DIM
GLA_LOW_RANK = 16
GLA_GATE_NORMALIZER = 16.0
D_FF = 5632
CHUNK = 64
NORM_EPS = 1e-6
LANES = 128
SUBLANES = 8
CUM_ROWS = 256
FFN_ROWS = 256
GLA_UNROLL = 8
GDN_GROUP = 8
NQ_ROWS = HEAD_DIM + CHUNK

_OFF_LR = 4 * HEAD_W
_OFF_GDN = _OFF_LR + 2 * GLA_LOW_RANK
_OFF_AB = _OFF_GDN + 4 * HEAD_W
_OFF_GATES = _OFF_AB + 4 * HEADS
N_GROUP = 4 * HEAD_W
assert _OFF_LR == N_GROUP and _OFF_AB - _OFF_GDN == N_GROUP and 2 * D_MODEL == N_GROUP
assert _OFF_GATES == 2 * _OFF_GDN
_SM_A_F, _SM_A_B, _SM_BETA_F, _SM_BETA_B = 32, 40, 48, 56
_CB_GLA_Q, _CB_GLA_K, _CB_GLA_V, _CB_GLA_GATE = 0, 8, 16, 24
_CB_GDN_Q, _CB_GDN_K, _CB_GDN_V, _CB_GDN_Z = 32, 40, 48, 56

VMEM_LIMIT = 56 * 1024 * 1024


def _dot(a, b):
    return jnp.dot(a, b, preferred_element_type=F32)


def _dot_nt(a, b):
    return lax.dot_general(a, b, (((1,), (1,)), ((), ())), preferred_element_type=F32)


def _dot_tn(a, b):
    return lax.dot_general(a, b, (((0,), (0,)), ((), ())), preferred_element_type=F32)


def _bf(x):
    return x.astype(BF16)


def _split(x):
    hi = x.astype(BF16)
    lo = (x - hi.astype(F32)).astype(BF16)
    return hi, lo


def _dot_exact_rhs(a_bf, b):
    hi, lo = _split(b)
    return _dot(a_bf, hi) + _dot(a_bf, lo)


def _dot_exact_lhs(a, b_bf):
    hi, lo = _split(a)
    return _dot(hi, b_bf) + _dot(lo, b_bf)


def _dot3(a, b):
    ah, al = _split(a)
    bh, bl = _split(b)
    return _dot(ah, bh) + (_dot(ah, bl) + _dot(al, bh))


def _sigmoid(x):
    return 0.5 * jnp.tanh(0.5 * x) + 0.5


def _silu_of_half(h):
    return h * jnp.tanh(h) + h


def _silu(x):
    return _silu_of_half(0.5 * x)


def _softplus(x):
    return jnp.maximum(x, 0.0) + jnp.log1p(jnp.exp(-jnp.abs(x)))


def _conv3(x, cw, row_before=None, row_after=None):
    n = x.shape[0]
    sub = lax.broadcasted_iota(jnp.int32, (SUBLANES, 1), 0)
    zero = jnp.zeros((1, x.shape[1]), x.dtype)
    x_prev = pltpu.roll(x, 1, 0)
    first = jnp.where(sub == 0, zero if row_before is None else row_before, x_prev[:SUBLANES, :])
    x_prev = jnp.concatenate([first, x_prev[SUBLANES:, :]], axis=0)
    x_next = pltpu.roll(x, n - 1, 0)
    last = jnp.where(sub == SUBLANES - 1, zero if row_after is None else row_after, x_next[n - SUBLANES:, :])
    x_next = jnp.concatenate([x_next[:n - SUBLANES, :], last], axis=0)
    return x_prev * cw[0:1, :] + x * cw[1:2, :] + x_next * cw[2:3, :]


def _rms_scale(x):
    return x * lax.rsqrt(jnp.mean(x * x, axis=-1, keepdims=True) + NORM_EPS)


def _repack_kernel(w_ref, o_ref):
    o_ref[...] = w_ref[...].astype(o_ref.dtype)


def _repack_w_in(w_t, group_stride, groups, width, tr=512):
    d = w_t.shape[1]
    return pl.pallas_call(
        _repack_kernel,
        grid=(groups, width // tr),
        in_specs=[pl.BlockSpec(
            (pl.Element(tr), pl.Element(d)),
            lambda g, i: (pl.multiple_of(g * group_stride + i * tr, math.gcd(group_stride, tr)), 0))],
        out_specs=pl.BlockSpec((None, tr, d), lambda g, i: (g, i, 0)),
        out_shape=jax.ShapeDtypeStruct((groups, width, d), BF16),
        compiler_params=pltpu.CompilerParams(
            dimension_semantics=("parallel", "parallel"), vmem_limit_bytes=VMEM_LIMIT),
        name="repack_w_in",
    )(w_t)


def _gdn_gates(sm, alog_row, dtb_row):
    tm = sm.shape[0]
    col = lax.broadcasted_iota(jnp.int32, (1, LANES), 1)
    is_g = (col >= _SM_A_F) & (col < _SM_BETA_F)
    is_fwd = col < _SM_A_B
    is_beta = (col >= _SM_BETA_F) & (col < _SM_BETA_B + HEADS)
    g = jnp.where(is_g, -jnp.exp(alog_row) * _softplus(sm + dtb_row), 0.0)
    r = lax.broadcasted_iota(jnp.int32, (CUM_ROWS, CUM_ROWS), 0)
    c = lax.broadcasted_iota(jnp.int32, (CUM_ROWS, CUM_ROWS), 1)
    same_chunk = (r // CHUNK) == (c // CHUNK)
    tril_bf = (same_chunk & (r >= c)).astype(F32).astype(BF16)
    triu_bf = (same_chunk & (r <= c)).astype(F32).astype(BF16)
    slabs = []
    for s in range(tm // CUM_ROWS):
        hi, lo = _split(g[s * CUM_ROWS:(s + 1) * CUM_ROWS])
        prefix = _dot(tril_bf, hi) + _dot(tril_bf, lo)
        suffix = _dot(triu_bf, hi) + _dot(triu_bf, lo)
        slabs.append(jnp.where(is_fwd, prefix, suffix))
    g_cum = jnp.concatenate(slabs, axis=0)
    return jnp.where(is_g, g_cum, jnp.where(is_beta, _sigmoid(sm), sm))


def _in_proj_kernel(x_ref, g_ref, w_ref, ws_ref, alog_ref, dtb_ref, p_ref, sm_ref, hn_ref):
    @pl.when(pl.program_id(1) == 0)
    def _():
        hn = (_rms_scale(x_ref[...]) * g_ref[...]).astype(BF16)
        hn_ref[...] = hn
        sm_ref[...] = _gdn_gates(_dot_nt(hn, _bf(ws_ref[...])), alog_ref[...], dtb_ref[...])

    p_ref[...] = _dot_nt(hn_ref[...], w_ref[...])


def _in_proj(x2, g, w_big, w_small, alog_row, dtb_row, tm=1024, tn=1024):
    m, d = x2.shape
    groups, n_group, _ = w_big.shape
    per_group = n_group // tn
    n = groups * n_group
    return pl.pallas_call(
        _in_proj_kernel,
        grid=(m // tm, n // tn),
        in_specs=[
            pl.BlockSpec((tm, d), lambda i, j: (i, 0)),
            pl.BlockSpec((1, d), lambda i, j: (0, 0)),
            pl.BlockSpec((None, tn, d), lambda i, j: (j // per_group, j % per_group, 0)),
            pl.BlockSpec((LANES, d), lambda i, j: (0, 0)),
            pl.BlockSpec((1, LANES), lambda i, j: (0, 0)),
            pl.BlockSpec((1, LANES), lambda i, j: (0, 0)),
        ],
        out_specs=[
            pl.BlockSpec((tm, tn), lambda i, j: (i, j)),
            pl.BlockSpec((tm, LANES), lambda i, j: (i, 0)),
        ],
        out_shape=[
            jax.ShapeDtypeStruct((m, n), F32),
            jax.ShapeDtypeStruct((m, LANES), F32),
        ],
        scratch_shapes=[pltpu.VMEM((tm, d), BF16)],
        compiler_params=pltpu.CompilerParams(
            dimension_semantics=("parallel", "arbitrary"), vmem_limit_bytes=VMEM_LIMIT),
        name="in_proj",
    )(x2, g, w_big, w_small, alog_row, dtb_row)


def _chunk_masks():
    r = lax.broadcasted_iota(jnp.int32, (CHUNK, CHUNK), 0)
    c = lax.broadcasted_iota(jnp.int32, (CHUNK, CHUNK), 1)
    return r, c


def _gated_head_norm(o, z, g):
    return (_rms_scale(o) * g) * _silu(z)


def _gla_chunks(chunks, states, q_ref, k_ref, v_ref, dirs, o_ref):
    pairs = []
    for u in range(len(chunks[0])):
        for d, (lg_ref, incl, tri_bf, edge) in enumerate(dirs):
            row0 = pl.multiple_of(chunks[d][u] * CHUNK, CHUNK)
            rows = pl.ds(row0, CHUNK)
            hi, lo = _split(lg_ref[rows, :])
            pairs.append(dict(d=d, row0=row0, rows=rows, incl=incl, edge=edge, hi=hi, lo=lo, tri=tri_bf))
    for p in pairs:
        p["g_cum"] = _dot(p["tri"], p["hi"]) + _dot(p["tri"], p["lo"])
    for p in pairs:
        g_cum = p["g_cum"]
        g_edge = g_cum[p["edge"]:p["edge"] + 1, :]
        k = k_ref[p["rows"], :]
        p["v_bf"] = _bf(v_ref[p["rows"], :])
        p["q_dec"] = _bf(q_ref[p["rows"], :] * (HEAD_DIM ** -0.5) * jnp.exp(g_cum))
        p["decay"] = jnp.exp(g_edge)
        p["scores"] = _dot_nt(p["q_dec"], _bf(k * jnp.exp(-g_cum)))
        p["upd"] = _dot_tn(p["v_bf"], _bf(k * jnp.exp(g_edge - g_cum)))
    for p in pairs:
        p["o"] = _dot(_bf(jnp.where(p["incl"], p["scores"], 0.0)), p["v_bf"])
    states = list(states)
    seq = q_ref.shape[0]
    for p in pairs:
        st = states[p["d"]]
        o_ref[pl.ds(p["d"] * seq + p["row0"], CHUNK), :] = p["o"] + _dot_nt(p["q_dec"], _bf(st))
        states[p["d"]] = st * p["decay"] + p["upd"]
    return tuple(states)


def _gla_kernel(q_ref, k_ref, v_ref, gate_ref, sm_ref, dwf_ref, dbf_ref, dwb_ref, dbb_ref, ng_ref,
                y_ref, lgf_ref, lgb_ref, o_ref):
    seq = q_ref.shape[0]
    n_chunks = seq // CHUNK
    n_groups = n_chunks // GLA_UNROLL
    group_rows = GLA_UNROLL * CHUNK

    def load_group(t):
        for lg_ref, dw_ref, db_ref, row0 in ((lgf_ref, dwf_ref, dbf_ref, t * group_rows),
                                             (lgb_ref, dwb_ref, dbb_ref, seq - (t + 1) * group_rows)):
            rows = pl.ds(pl.multiple_of(row0, group_rows), group_rows)
            pre = _dot(_bf(sm_ref[rows, :]), dw_ref[...]) + db_ref[...]
            lg_ref[rows, :] = -_softplus(-pre) / GLA_GATE_NORMALIZER

    r, c = _chunk_masks()
    lower, upper = r >= c, r <= c
    tril_bf = lower.astype(F32).astype(BF16)
    triu_bf = upper.astype(F32).astype(BF16)

    dirs = ((lgf_ref, lower, tril_bf, CHUNK - 1), (lgb_ref, upper, triu_bf, 0))

    def advance(t, carry):
        fwd = [t * GLA_UNROLL + u for u in range(GLA_UNROLL)]
        bwd = [n_chunks - 1 - n for n in fwd]
        return _gla_chunks((fwd, bwd), carry, q_ref, k_ref, v_ref, dirs, o_ref)

    def body(t, carry):
        carry = advance(t, carry)
        load_group(t + 1)
        return carry

    zero = jnp.zeros((HEAD_DIM, HEAD_DIM), F32)
    load_group(0)
    carry = lax.fori_loop(0, n_groups - 1, body, (zero, zero))
    advance(n_groups - 1, carry)
    o = o_ref[0:seq, :] + o_ref[seq:2 * seq, :]
    y_ref[...] = _gated_head_norm(o, gate_ref[...], ng_ref[...]).astype(y_ref.dtype)


def _gla(p, sm, dw_f, db_f, dw_b, db_b, norm_g, batch, seq):
    def col(base):
        return pl.BlockSpec((seq, LANES), lambda b, h: (b, base + h))

    head_vec = pl.BlockSpec((1, LANES), lambda b, h: (0, h))
    head_mat = pl.BlockSpec((LANES, LANES), lambda b, h: (0, h))
    shared_vec = pl.BlockSpec((1, LANES), lambda b, h: (0, 0))
    return pl.pallas_call(
        _gla_kernel,
        grid=(batch, HEADS),
        in_specs=[col(_CB_GLA_Q), col(_CB_GLA_K), col(_CB_GLA_V), col(_CB_GLA_GATE),
                  pl.BlockSpec((seq, LANES), lambda b, h: (b, 0)),
                  head_mat, head_vec, head_mat, head_vec, shared_vec],
        out_specs=pl.BlockSpec((seq, LANES), lambda b, h: (b, h)),
        out_shape=jax.ShapeDtypeStruct((batch * seq, HEAD_W), BF16),
        scratch_shapes=[pltpu.VMEM((seq, LANES), F32)] * 2 + [pltpu.VMEM((2 * seq, LANES), F32)],
        compiler_params=pltpu.CompilerParams(
            dimension_semantics=("parallel", "arbitrary"), vmem_limit_bytes=VMEM_LIMIT),
        name="gla",
    )(p, p, p, p, sm, dw_f, db_f, dw_b, db_b, norm_g)


def _unit_tri_inverses(l_mats, r, c, eye):
    blk16 = (r // 16) == (c // 16)
    blk32 = (r // 32) == (c // 32)
    a = [-jnp.where(blk16, l, 0.0) for l in l_mats]
    a_bf = [_bf(x) for x in a]
    p2_bf = [_bf(_dot(x, x)) for x in a_bf]
    yield
    t = [eye + x for x in a]
    t = [x + _dot(_bf(x), p) for x, p in zip(t, p2_bf)]
    p4_bf = [_bf(_dot(p, p)) for p in p2_bf]
    yield
    t = [x + _dot(_bf(x), p) for x, p in zip(t, p4_bf)]
    p8_bf = [_bf(_dot(p, p)) for p in p4_bf]
    yield
    t = [x + _dot(_bf(x), p) for x, p in zip(t, p8_bf)]
    yield
    for off_diag in (blk32 & ~blk16, ~blk32):
        t_bf = [_bf(x) for x in t]
        e_t = [_bf(_dot(_bf(jnp.where(off_diag, l, 0.0)), x)) for l, x in zip(l_mats, t_bf)]
        yield
        t = [x - _dot(xb, e) for x, xb, e in zip(t, t_bf, e_t)]
        yield
    return t


def _interleave(stages, actions):
    actions = list(actions)
    for _ in stages:
        if actions:
            actions.pop(0)()
    for act in actions:
        act()


def _gdn_prepare(chunks, qs_ref, ks_ref, vs_ref, dirs, o_ref, nq_ref, b_ref, d_ref, r, c, eye, n_chunks):
    seq = n_chunks * CHUNK
    pairs = []
    for u in range(len(chunks[0])):
        for d, (g_ref, beta_ref, incl, strict, edge) in enumerate(dirs):
            n = chunks[d][u]
            row0 = pl.multiple_of(n * CHUNK, CHUNK)
            rows = pl.ds(row0, CHUNK)
            pairs.append(dict(n=n, d=d, row0=row0, q=qs_ref[rows, :], k=ks_ref[rows, :], v=vs_ref[rows, :],
                              g_cum=g_ref[rows, :], beta=beta_ref[rows, :],
                              incl=incl, strict=strict, edge=edge))
    for p in pairs:
        k_bf = _bf(p["k"])
        p["kk"] = _dot_nt(k_bf, k_bf)
        p["qk"] = _dot_nt(_bf(p["q"]), k_bf)
    yield
    for p in pairs:
        g_cum, beta = p["g_cum"], p["beta"]
        g_cum_t = g_cum.T
        diff = g_cum[:, :CHUNK] - g_cum_t[:CHUNK, :]
        decay = jnp.where(p["incl"], jnp.exp(jnp.where(p["incl"], diff, 0.0)), 0.0)
        p["l_mat"] = jnp.where(p["strict"], p["kk"], 0.0) * decay * beta[:, :CHUNK]
        p["e_g"] = jnp.exp(g_cum)
        p["g_edge"] = g_cum[p["edge"]:p["edge"] + 1, :]
        p["rhs"] = _bf(jnp.concatenate([p["k"] * (beta * p["e_g"]), p["v"] * beta], axis=1))
        p["attn"] = _bf(p["qk"] * decay)
        p["k_tail_t"] = _bf(p["k"].T * jnp.exp(p["g_edge"][:, :CHUNK] - g_cum_t))
    t_inv = yield from _unit_tri_inverses([p["l_mat"] for p in pairs], r, c, eye)
    wu_bf = [_bf(_dot(_bf(t), p["rhs"])) for t, p in zip(t_inv, pairs)]
    yield
    aw = [_dot(p["attn"], x) for p, x in zip(pairs, wu_bf)]
    kw = [_dot(p["k_tail_t"], x) for p, x in zip(pairs, wu_bf)]
    for p, aw_p, kw_p in zip(pairs, aw, kw):
        slot = p["d"] * n_chunks + p["n"]
        nq_rows = pl.multiple_of(slot * NQ_ROWS, NQ_ROWS)
        nq_ref[pl.ds(nq_rows, HEAD_DIM), :] = _bf(-kw_p[:, :LANES])
        nq_ref[pl.ds(nq_rows + HEAD_DIM, CHUNK), :] = _bf(p["q"] * p["e_g"] - aw_p[:, :LANES])
        b_ref[pl.ds(pl.multiple_of(slot * HEAD_DIM, HEAD_DIM), HEAD_DIM), :] = kw_p[:, LANES:]
        d_ref[pl.ds(slot, 1), :] = jnp.exp(p["g_edge"])
        o_ref[pl.ds(p["d"] * seq + p["row0"], CHUNK), :] = aw_p[:, LANES:]


def _gdn_step_actions(chunks, states, o_ref, nq_ref, b_ref, d_ref, n_chunks):
    seq = n_chunks * CHUNK

    def step(u):
        for d in range(2):
            n = chunks[d][u]
            slot = d * n_chunks + n
            nq = nq_ref[pl.ds(pl.multiple_of(slot * NQ_ROWS, NQ_ROWS), NQ_ROWS), :]
            res = _dot(nq, _bf(states[d]))
            o_ref[pl.ds((2 + d) * seq + pl.multiple_of(n * CHUNK, CHUNK), CHUNK), :] = res[HEAD_DIM:, :]
            b_mat = b_ref[pl.ds(pl.multiple_of(slot * HEAD_DIM, HEAD_DIM), HEAD_DIM), :]
            states[d] = states[d] * d_ref[pl.ds(slot, 1), :] + (res[:HEAD_DIM, :] + b_mat)

    return [functools.partial(step, u) for u in range(len(chunks[0]))]


def _gdn_kernel(q_ref, k_ref, v_ref, z_ref, sm_ref, cq_ref, ck_ref, cv_ref, ng_ref,
                y_ref, qs_ref, ks_ref, vs_ref, gf_ref, gb_ref, bf_ref, bb_ref, o_ref, nq_ref, b_ref, d_ref):
    h = pl.program_id(1)
    seq = q_ref.shape[0]
    n_chunks = seq // CHUNK

    n_groups = n_chunks // GDN_GROUP
    slab_rows = GDN_GROUP * CHUNK
    sel_row = lax.broadcasted_iota(jnp.int32, (LANES, 4 * LANES), 0)
    sel_grp = lax.broadcasted_iota(jnp.int32, (LANES, 4 * LANES), 1) // LANES
    sel = (sel_row == _SM_A_F + HEADS * sel_grp + h).astype(F32).astype(BF16)

    def l2_norm(t, scale=1.0):
        return t * (lax.rsqrt(jnp.sum(t * t, axis=-1, keepdims=True) + NORM_EPS) * scale)

    def load_slab(s):
        r0, r1 = s * slab_rows, (s + 1) * slab_rows

        def conv_silu(x_ref, c_ref):
            before = x_ref[r0 - 1:r0, :] if s > 0 else None
            after = x_ref[r1:r1 + 1, :] if r1 < seq else None
            return _silu_of_half(_conv3(x_ref[r0:r1, :], 0.5 * c_ref[...], before, after))

        qs_ref[r0:r1, :] = l2_norm(conv_silu(q_ref, cq_ref), HEAD_DIM ** -0.5)
        ks_ref[r0:r1, :] = l2_norm(conv_silu(k_ref, ck_ref))
        vs_ref[r0:r1, :] = conv_silu(v_ref, cv_ref)
        sm_hi, sm_lo = _split(sm_ref[r0:r1, :])
        cols = _dot(sm_hi, sel) + _dot(sm_lo, sel)
        for i, ref in enumerate((gf_ref, gb_ref, bf_ref, bb_ref)):
            ref[r0:r1, :] = cols[:, i * LANES:(i + 1) * LANES]

    r, c = _chunk_masks()
    eye = (r == c).astype(F32)
    dirs = ((gf_ref, bf_ref, r >= c, r > c, CHUNK - 1),
            (gb_ref, bb_ref, r <= c, r < c, 0))

    def group(t):
        fwd = [t * GDN_GROUP + u for u in range(GDN_GROUP)]
        return fwd, [n_chunks - 1 - n for n in fwd]

    def prepare(t):
        return _gdn_prepare(group(t), qs_ref, ks_ref, vs_ref, dirs, o_ref, nq_ref, b_ref, d_ref,
                            r, c, eye, n_chunks)

    def advance(t, states):
        return _gdn_step_actions(group(t), states, o_ref, nq_ref, b_ref, d_ref, n_chunks)

    def body(t, states):
        states = list(states)
        _interleave(prepare(t), advance(t - 1, states))
        return tuple(states)

    zero = jnp.zeros((HEAD_DIM, HEAD_DIM), F32)
    load_slab(0)
    load_slab(n_groups - 1)
    _interleave(prepare(0), [functools.partial(load_slab, s) for s in range(1, n_groups - 1)])
    states = list(lax.fori_loop(1, n_groups, body, (zero, zero)))
    _interleave(iter(()), advance(n_groups - 1, states))
    o = o_ref[0:seq, :] + o_ref[seq:2 * seq, :] + o_ref[2 * seq:3 * seq, :] + o_ref[3 * seq:4 * seq, :]
    y_ref[...] = _gated_head_norm(o, z_ref[...], ng_ref[...]).astype(y_ref.dtype)


def _gdn(p, sm, conv_w, norm_g, batch, seq):
    def col(base):
        return pl.BlockSpec((seq, LANES), lambda b, h: (b, base + h))

    def conv_col(base):
        return pl.BlockSpec((3, LANES), lambda b, h: (0, base + h))

    slots = 2 * (seq // CHUNK)
    return pl.pallas_call(
        _gdn_kernel,
        grid=(batch, HEADS),
        in_specs=[col(_CB_GDN_Q), col(_CB_GDN_K), col(_CB_GDN_V), col(_CB_GDN_Z),
                  pl.BlockSpec((seq, LANES), lambda b, h: (b, 0)),
                  conv_col(0), conv_col(HEADS), conv_col(2 * HEADS),
                  pl.BlockSpec((1, LANES), lambda b, h: (0, 0))],
        out_specs=pl.BlockSpec((seq, LANES), lambda b, h: (b, h)),
        out_shape=jax.ShapeDtypeStruct((batch * seq, HEAD_W), BF16),
        scratch_shapes=[pltpu.VMEM((seq, LANES), F32)] * 7 + [
            pltpu.VMEM((4 * seq, LANES), F32),
            pltpu.VMEM((slots * NQ_ROWS, LANES), BF16),
            pltpu.VMEM((slots * HEAD_DIM, LANES), F32),
            pltpu.VMEM((slots, LANES), F32)],
        compiler_params=pltpu.CompilerParams(
            dimension_semantics=("parallel", "arbitrary"), vmem_limit_bytes=VMEM_LIMIT),
        name="gdn",
    )(p, p, p, p, sm, conv_w, conv_w, conv_w, norm_g)


def _merge_out_kernel(yg_ref, yd_ref, gg_ref, gd_ref, x_ref, wbg_ref, wbd_ref, wo_ref, n2_ref,
                      x1_ref, h2_ref):
    merged = (_sigmoid(gg_ref[...]) * _dot(yg_ref[...], wbg_ref[...])
              + _sigmoid(gd_ref[...]) * _dot(yd_ref[...], wbd_ref[...]))
    x1 = x_ref[...] + _dot(merged.astype(BF16), wo_ref[...])
    x1_ref[...] = x1
    h2_ref[...] = (_rms_scale(x1) * n2_ref[...]).astype(h2_ref.dtype)


def _merge_out(y_gla, y_gdn, p, x2, w_bg, w_bd, w_o, n2, tm=256):
    m, d = x2.shape
    gate_cb = (8 * HEAD_W) // d

    def resident(shape):
        return pl.BlockSpec(shape, lambda i: (0, 0), pipeline_mode=pl.Buffered(1))

    return pl.pallas_call(
        _merge_out_kernel,
        grid=(m // tm,),
        in_specs=[
            pl.BlockSpec((tm, HEAD_W), lambda i: (i, 0)),
            pl.BlockSpec((tm, HEAD_W), lambda i: (i, 0)),
            pl.BlockSpec((tm, d), lambda i: (i, gate_cb)),
            pl.BlockSpec((tm, d), lambda i: (i, gate_cb + 1)),
            pl.BlockSpec((tm, d), lambda i: (i, 0)),
            resident((HEAD_W, d)), resident((HEAD_W, d)), resident((d, d)), resident((1, d)),
        ],
        out_specs=[pl.BlockSpec((tm, d), lambda i: (i, 0)), pl.BlockSpec((tm, d), lambda i: (i, 0))],
        out_shape=[jax.ShapeDtypeStruct((m, d), F32), jax.ShapeDtypeStruct((m, d), BF16)],
        compiler_params=pltpu.CompilerParams(
            dimension_semantics=("parallel",), vmem_limit_bytes=VMEM_LIMIT),
        name="merge_out",
    )(y_gla, y_gdn, p, p, x2, w_bg, w_bd, w_o, n2)


def _ffn_up_kernel(h_ref, wg_ref, wv_ref, cg_ref, cv_ref, bg_ref, bv_ref, act_ref, u_ref):
    seq = h_ref.shape[0]
    n_slabs = seq // FFN_ROWS
    lo, hi = SUBLANES, SUBLANES + FFN_ROWS
    params = ((wg_ref, 0.5 * cg_ref[...], 0.5 * bg_ref[...]), (wv_ref, cv_ref[...], bv_ref[...]))
    zero_row = jnp.zeros((1, act_ref.shape[1]), F32)

    def project(s):
        h = h_ref[s * FFN_ROWS:(s + 1) * FFN_ROWS, :]
        edges = []
        for i, (w_ref, _, _) in enumerate(params):
            u = _dot(h, w_ref[...])
            u_ref[s % 2, i, lo:hi, :] = u
            edges.append((u[:1, :], u[FFN_ROWS - 1:, :]))
        return edges

    def finish(s, before, after):
        out = []
        for i, (_, cw, b) in enumerate(params):
            u_ref[s % 2, i, lo - 1:lo, :] = zero_row if before is None else before[i][1]
            u_ref[s % 2, i, hi:hi + 1, :] = zero_row if after is None else after[i][0]
            out.append(u_ref[s % 2, i, lo - 1:hi - 1, :] * cw[0:1, :] + u_ref[s % 2, i, lo:hi, :] * cw[1:2, :]
                       + u_ref[s % 2, i, lo + 1:hi + 1, :] * cw[2:3, :] + b)
        act_ref[s * FFN_ROWS:(s + 1) * FFN_ROWS, :] = (_silu_of_half(out[0]) * out[1]).astype(act_ref.dtype)

    edges = [project(0)]
    for s in range(1, n_slabs):
        edges.append(project(s))
        finish(s - 1, edges[s - 2] if s >= 2 else None, edges[s])
    finish(n_slabs - 1, edges[-2] if n_slabs >= 2 else None, None)


def _ffn_up(h2, w_up, conv_w, conv_b, batch, seq, tn=512):
    d = h2.shape[1]
    nj = D_FF // tn
    return pl.pallas_call(
        _ffn_up_kernel,
        grid=(batch, nj),
        in_specs=[
            pl.BlockSpec((seq, d), lambda b, j: (b, 0)),
            pl.BlockSpec((d, tn), lambda b, j: (0, j)),
            pl.BlockSpec((d, tn), lambda b, j: (0, nj + j)),
            pl.BlockSpec((3, tn), lambda b, j: (0, j)),
            pl.BlockSpec((3, tn), lambda b, j: (0, nj + j)),
            pl.BlockSpec((1, tn), lambda b, j: (0, j)),
            pl.BlockSpec((1, tn), lambda b, j: (0, nj + j)),
        ],
        out_specs=pl.BlockSpec((seq, tn), lambda b, j: (b, j)),
        out_shape=jax.ShapeDtypeStruct((batch * seq, D_FF), BF16),
        scratch_shapes=[pltpu.VMEM((2, 2, FFN_ROWS + 2 * SUBLANES, tn), F32)],
        compiler_params=pltpu.CompilerParams(
            dimension_semantics=("parallel", "arbitrary"), vmem_limit_bytes=VMEM_LIMIT),
        name="ffn_up",
    )(h2, w_up, w_up, conv_w, conv_w, conv_b, conv_b)


def _ffn_down_kernel(act_ref, w_ref, x1_ref, g_ref, out_ref, *, final_norm):
    x2 = x1_ref[...] + _dot(act_ref[...], w_ref[...])
    out_ref[...] = _rms_scale(x2) * g_ref[...] if final_norm else x2


def _ffn_down(act, w_down, x1, g, final_norm, tm=256):
    m, d = x1.shape
    return pl.pallas_call(
        functools.partial(_ffn_down_kernel, final_norm=final_norm),
        grid=(m // tm,),
        in_specs=[
            pl.BlockSpec((tm, D_FF), lambda i: (i, 0)),
            pl.BlockSpec((D_FF, d), lambda i: (0, 0), pipeline_mode=pl.Buffered(1)),
            pl.BlockSpec((tm, d), lambda i: (i, 0)),
            pl.BlockSpec((1, d), lambda i: (0, 0)),
        ],
        out_specs=pl.BlockSpec((tm, d), lambda i: (i, 0)),
        out_shape=jax.ShapeDtypeStruct((m, d), F32),
        compiler_params=pltpu.CompilerParams(
            dimension_semantics=("parallel",), vmem_limit_bytes=VMEM_LIMIT),
        name="ffn_down",
    )(act, w_down, x1, g)


def _pad_rows(w, row0, rows):
    return jnp.zeros((rows, w.shape[1]), w.dtype).at[row0:row0 + w.shape[0]].set(w)


def kernel(x, norm1_g, w_in, gla_decay_w_f, gla_decay_b_f, gla_decay_w_b, gla_decay_b_b, gla_norm_g,
           gdn_conv_w, gdn_a_log_f, gdn_dt_bias_f, gdn_a_log_b, gdn_dt_bias_b, gdn_norm_g,
           w_branch_gla, w_branch_gdn, w_out, norm2_g, w_up, ffn_conv_w, ffn_conv_b, w_down,
           final_norm_g):
    batch, seq, d = x.shape
    depth = w_in.shape[0]
    x2 = x.reshape(batch * seq, d)
    for l in range(depth):
        wl_t = jnp.swapaxes(w_in, 1, 2)[l]
        w_big = _repack_w_in(wl_t, _OFF_GDN, 3, N_GROUP)
        w_small = jnp.concatenate(
            [wl_t[_OFF_LR:_OFF_GDN], wl_t[_OFF_AB:_OFF_GATES],
             jnp.zeros((LANES - 2 * GLA_LOW_RANK - 4 * HEADS, d), wl_t.dtype)], axis=0)
        pad = jnp.zeros((LANES - _SM_BETA_F,), F32)

        def gate_row(fwd, bwd):
            return jnp.concatenate([jnp.zeros((_SM_A_F,), F32), fwd, bwd, pad])[None, :]

        p, sm = _in_proj(x2, norm1_g[l][None, :], w_big, w_small,
                         gate_row(gdn_a_log_f[l], gdn_a_log_b[l]),
                         gate_row(gdn_dt_bias_f[l], gdn_dt_bias_b[l]))

        y_gla = _gla(p, sm,
                     _pad_rows(gla_decay_w_f[l], 0, LANES).astype(BF16), gla_decay_b_f[l][None, :],
                     _pad_rows(gla_decay_w_b[l], GLA_LOW_RANK, LANES).astype(BF16), gla_decay_b_b[l][None, :],
                     gla_norm_g[l][None, :], batch, seq)
        y_gdn = _gdn(p, sm, gdn_conv_w[l], gdn_norm_g[l][None, :], batch, seq)

        x2, h2 = _merge_out(y_gla, y_gdn, p, x2,
                            w_branch_gla[l].astype(BF16), w_branch_gdn[l].astype(BF16),
                            w_out[l].astype(BF16), norm2_g[l][None, :])
        act = _ffn_up(h2, w_up[l].astype(BF16), ffn_conv_w[l], ffn_conv_b[l][None, :], batch, seq)
        x2 = _ffn_down(act, w_down[l].astype(BF16), x2, final_norm_g[None, :], l == depth - 1)
    return x2.reshape(batch, seq, d)
```

```python
import functools
import math

import jax
import jax.numpy as jnp
from jax import lax
from jax.experimental import pallas as pl
from jax.experimental.pallas import tpu as pltpu

F32 = jnp.float32
BF16 = jnp.bfloat16

D_MODEL = 2048
HEADS = 8
HEAD_DIM = 128
HEAD_W = HEADS * HEAD_DIM
GLA_LOW_RANK = 16
GLA_GATE_NORMALIZER = 16.0
D_FF = 5632
CHUNK = 64
NORM_EPS = 1e-6
LANES = 128
SUBLANES = 8
CUM_ROWS = 256
FFN_ROWS = 256
MIX_GROUP = 8
NQ_ROWS = HEAD_DIM + CHUNK

_OFF_LR = 4 * HEAD_W
_OFF_GDN = _OFF_LR + 2 * GLA_LOW_RANK
_OFF_AB = _OFF_GDN + 4 * HEAD_W
_OFF_GATES = _OFF_AB + 4 * HEADS
N_GROUP = 4 * HEAD_W
assert _OFF_LR == N_GROUP and _OFF_AB - _OFF_GDN == N_GROUP and 2 * D_MODEL == N_GROUP
assert _OFF_GATES == 2 * _OFF_GDN
_SM_A_F, _SM_A_B, _SM_BETA_F, _SM_BETA_B = 32, 40, 48, 56
_CB_GLA_Q, _CB_GLA_K, _CB_GLA_V, _CB_GLA_GATE = 0, 8, 16, 24
_CB_GDN_Q, _CB_GDN_K, _CB_GDN_V, _CB_GDN_Z = 32, 40, 48, 56

VMEM_LIMIT = 56 * 1024 * 1024


def _dot(a, b):
    return jnp.dot(a, b, preferred_element_type=F32)


def _dot_nt(a, b):
    return lax.dot_general(a, b, (((1,), (1,)), ((), ())), preferred_element_type=F32)


def _dot_tn(a, b):
    return lax.dot_general(a, b, (((0,), (0,)), ((), ())), preferred_element_type=F32)


def _bf(x):
    return x.astype(BF16)


def _split(x):
    hi = x.astype(BF16)
    lo = (x - hi.astype(F32)).astype(BF16)
    return hi, lo


def _dot_exact_rhs(a_bf, b):
    hi, lo = _split(b)
    return _dot(a_bf, hi) + _dot(a_bf, lo)


def _dot_exact_lhs(a, b_bf):
    hi, lo = _split(a)
    return _dot(hi, b_bf) + _dot(lo, b_bf)


def _dot3(a, b):
    ah, al = _split(a)
    bh, bl = _split(b)
    return _dot(ah, bh) + (_dot(ah, bl) + _dot(al, bh))


def _sigmoid(x):
    return 0.5 * jnp.tanh(0.5 * x) + 0.5


def _silu_of_half(h):
    return h * jnp.tanh(h) + h


def _silu(x):
    return _silu_of_half(0.5 * x)


def _softplus(x):
    return jnp.maximum(x, 0.0) + jnp.log1p(jnp.exp(-jnp.abs(x)))


def _conv3(x, cw, row_before=None, row_after=None):
    n = x.shape[0]
    sub = lax.broadcasted_iota(jnp.int32, (SUBLANES, 1), 0)
    zero = jnp.zeros((1, x.shape[1]), x.dtype)
    x_prev = pltpu.roll(x, 1, 0)
    first = jnp.where(sub == 0, zero if row_before is None else row_before, x_prev[:SUBLANES, :])
    x_prev = jnp.concatenate([first, x_prev[SUBLANES:, :]], axis=0)
    x_next = pltpu.roll(x, n - 1, 0)
    last = jnp.where(sub == SUBLANES - 1, zero if row_after is None else row_after, x_next[n - SUBLANES:, :])
    x_next = jnp.concatenate([x_next[:n - SUBLANES, :], last], axis=0)
    return x_prev * cw[0:1, :] + x * cw[1:2, :] + x_next * cw[2:3, :]


def _rms_scale(x):
    return x * lax.rsqrt(jnp.mean(x * x, axis=-1, keepdims=True) + NORM_EPS)


def _repack_kernel(w_ref, o_ref):
    o_ref[...] = w_ref[...].astype(o_ref.dtype)


def _repack_w_in(w_t, group_stride, groups, width, tr=512):
    d = w_t.shape[1]
    return pl.pallas_call(
        _repack_kernel,
        grid=(groups, width // tr),
        in_specs=[pl.BlockSpec(
            (pl.Element(tr), pl.Element(d)),
            lambda g, i: (pl.multiple_of(g * group_stride + i * tr, math.gcd(group_stride, tr)), 0))],
        out_specs=pl.BlockSpec((None, tr, d), lambda g, i: (g, i, 0)),
        out_shape=jax.ShapeDtypeStruct((groups, width, d), BF16),
        compiler_params=pltpu.CompilerParams(
            dimension_semantics=("parallel", "parallel"), vmem_limit_bytes=VMEM_LIMIT),
        name="repack_w_in",
    )(w_t)


def _gdn_gates(sm, alog_row, dtb_row):
    tm = sm.shape[0]
    col = lax.broadcasted_iota(jnp.int32, (1, LANES), 1)
    is_g = (col >= _SM_A_F) & (col < _SM_BETA_F)
    is_fwd = col < _SM_A_B
    is_beta = (col >= _SM_BETA_F) & (col < _SM_BETA_B + HEADS)
    g = jnp.where(is_g, -jnp.exp(alog_row) * _softplus(sm + dtb_row), 0.0)
    r = lax.broadcasted_iota(jnp.int32, (CUM_ROWS, CUM_ROWS), 0)
    c = lax.broadcasted_iota(jnp.int32, (CUM_ROWS, CUM_ROWS), 1)
    same_chunk = (r // CHUNK) == (c // CHUNK)
    tril_bf = (same_chunk & (r >= c)).astype(F32).astype(BF16)
    triu_bf = (same_chunk & (r <= c)).astype(F32).astype(BF16)
    slabs = []
    for s in range(tm // CUM_ROWS):
        hi, lo = _split(g[s * CUM_ROWS:(s + 1) * CUM_ROWS])
        prefix = _dot(tril_bf, hi) + _dot(tril_bf, lo)
        suffix = _dot(triu_bf, hi) + _dot(triu_bf, lo)
        slabs.append(jnp.where(is_fwd, prefix, suffix))
    g_cum = jnp.concatenate(slabs, axis=0)
    return jnp.where(is_g, g_cum, jnp.where(is_beta, _sigmoid(sm), sm))


def _in_proj_kernel(x_ref, g_ref, w_ref, ws_ref, alog_ref, dtb_ref, p_ref, sm_ref, hn_ref):
    @pl.when(pl.program_id(1) == 0)
    def _():
        hn = (_rms_scale(x_ref[...]) * g_ref[...]).astype(BF16)
        hn_ref[...] = hn
        sm_ref[...] = _gdn_gates(_dot_nt(hn, _bf(ws_ref[...])), alog_ref[...], dtb_ref[...])

    p_ref[...] = _dot_nt(hn_ref[...], w_ref[...])


def _in_proj(x2, g, w_big, w_small, alog_row, dtb_row, tm=1024, tn=1024):
    m, d = x2.shape
    groups, n_group, _ = w_big.shape
    per_group = n_group // tn
    n = groups * n_group
    return pl.pallas_call(
        _in_proj_kernel,
        grid=(m // tm, n // tn),
        in_specs=[
            pl.BlockSpec((tm, d), lambda i, j: (i, 0)),
            pl.BlockSpec((1, d), lambda i, j: (0, 0)),
            pl.BlockSpec((None, tn, d), lambda i, j: (j // per_group, j % per_group, 0)),
            pl.BlockSpec((LANES, d), lambda i, j: (0, 0)),
            pl.BlockSpec((1, LANES), lambda i, j: (0, 0)),
            pl.BlockSpec((1, LANES), lambda i, j: (0, 0)),
        ],
        out_specs=[
            pl.BlockSpec((tm, tn), lambda i, j: (i, j)),
            pl.BlockSpec((tm, LANES), lambda i, j: (i, 0)),
        ],
        out_shape=[
            jax.ShapeDtypeStruct((m, n), F32),
            jax.ShapeDtypeStruct((m, LANES), F32),
        ],
        scratch_shapes=[pltpu.VMEM((tm, d), BF16)],
        compiler_params=pltpu.CompilerParams(
            dimension_semantics=("parallel", "arbitrary"), vmem_limit_bytes=VMEM_LIMIT),
        name="in_proj",
    )(x2, g, w_big, w_small, alog_row, dtb_row)


def _chunk_masks():
    r = lax.broadcasted_iota(jnp.int32, (CHUNK, CHUNK), 0)
    c = lax.broadcasted_iota(jnp.int32, (CHUNK, CHUNK), 1)
    return r, c


def _gated_head_norm(o, z, g):
    return (_rms_scale(o) * g) * _silu(z)


def _gla_stages(chunks, states, q_ref, k_ref, v_ref, dirs, o_ref):
    pairs = []
    for u in range(len(chunks[0])):
        for d, (lg_ref, incl, tri_bf, edge) in enumerate(dirs):
            row0 = pl.multiple_of(chunks[d][u] * CHUNK, CHUNK)
            rows = pl.ds(row0, CHUNK)
            hi, lo = _split(lg_ref[rows, :])
            pairs.append(dict(d=d, row0=row0, rows=rows, incl=incl, edge=edge, hi=hi, lo=lo, tri=tri_bf))
    for p in pairs:
        p["g_cum"] = _dot(p["tri"], p["hi"]) + _dot(p["tri"], p["lo"])
    yield
    for p in pairs:
        g_cum = p["g_cum"]
        g_edge = g_cum[p["edge"]:p["edge"] + 1, :]
        k = k_ref[p["rows"], :]
        p["v_bf"] = _bf(v_ref[p["rows"], :])
        p["q_dec"] = _bf(q_ref[p["rows"], :] * (HEAD_DIM ** -0.5) * jnp.exp(g_cum))
        p["decay"] = jnp.exp(g_edge)
        p["scores"] = _dot_nt(p["q_dec"], _bf(k * jnp.exp(-g_cum)))
        p["upd"] = _dot_tn(p["v_bf"], _bf(k * jnp.exp(g_edge - g_cum)))
    yield
    for p in pairs:
        p["o"] = _dot(_bf(jnp.where(p["incl"], p["scores"], 0.0)), p["v_bf"])
    yield
    seq = q_ref.shape[0]
    for p in pairs:
        st = states[p["d"]]
        o_ref[pl.ds(p["d"] * seq + p["row0"], CHUNK), :] = p["o"] + _dot_nt(p["q_dec"], _bf(st))
        states[p["d"]] = st * p["decay"] + p["upd"]


GLA_STAGES = 4


def _unit_tri_inverses(l_mats, r, c, eye):
    blk16 = (r // 16) == (c // 16)
    blk32 = (r // 32) == (c // 32)
    a = [-jnp.where(blk16, l, 0.0) for l in l_mats]
    a_bf = [_bf(x) for x in a]
    p2_bf = [_bf(_dot(x, x)) for x in a_bf]
    yield
    t = [eye + x for x in a]
    t = [x + _dot(_bf(x), p) for x, p in zip(t, p2_bf)]
    p4_bf = [_bf(_dot(p, p)) for p in p2_bf]
    yield
    t = [x + _dot(_bf(x), p) for x, p in zip(t, p4_bf)]
    p8_bf = [_bf(_dot(p, p)) for p in p4_bf]
    yield
    t = [x + _dot(_bf(x), p) for x, p in zip(t, p8_bf)]
    yield
    for off_diag in (blk32 & ~blk16, ~blk32):
        t_bf = [_bf(x) for x in t]
        e_t = [_bf(_dot(_bf(jnp.where(off_diag, l, 0.0)), x)) for l, x in zip(l_mats, t_bf)]
        yield
        t = [x - _dot(xb, e) for x, xb, e in zip(t, t_bf, e_t)]
        yield
    return t


def _interleave(stages, actions):
    actions = list(actions)
    for _ in stages:
        if actions:
            actions.pop(0)()
    for act in actions:
        act()


def _gdn_prepare(chunks, qs_ref, ks_ref, vs_ref, dirs, o_ref, nq_ref, b_ref, d_ref, r, c, eye, n_chunks):
    seq = n_chunks * CHUNK
    pairs = []
    for u in range(len(chunks[0])):
        for d, (g_ref, beta_ref, incl, strict, edge) in enumerate(dirs):
            n = chunks[d][u]
            row0 = pl.multiple_of(n * CHUNK, CHUNK)
            rows = pl.ds(row0, CHUNK)
            pairs.append(dict(n=n, d=d, row0=row0, q=qs_ref[rows, :], k=ks_ref[rows, :], v=vs_ref[rows, :],
                              g_cum=g_ref[rows, :], beta=beta_ref[rows, :],
                              incl=incl, strict=strict, edge=edge))
    for p in pairs:
        k_bf = _bf(p["k"])
        p["kk"] = _dot_nt(k_bf, k_bf)
        p["qk"] = _dot_nt(_bf(p["q"]), k_bf)
    yield
    for p in pairs:
        g_cum, beta = p["g_cum"], p["beta"]
        g_cum_t = g_cum.T
        diff = g_cum[:, :CHUNK] - g_cum_t[:CHUNK, :]
        decay = jnp.where(p["incl"], jnp.exp(jnp.where(p["incl"], diff, 0.0)), 0.0)
        p["l_mat"] = jnp.where(p["strict"], p["kk"], 0.0) * decay * beta[:, :CHUNK]
        p["e_g"] = jnp.exp(g_cum)
        p["g_edge"] = g_cum[p["edge"]:p["edge"] + 1, :]
        p["rhs"] = _bf(jnp.concatenate([p["k"] * (beta * p["e_g"]), p["v"] * beta], axis=1))
        p["attn"] = _bf(p["qk"] * decay)
        p["k_tail_t"] = _bf(p["k"].T * jnp.exp(p["g_edge"][:, :CHUNK] - g_cum_t))
    t_inv = yield from _unit_tri_inverses([p["l_mat"] for p in pairs], r, c, eye)
    wu_bf = [_bf(_dot(_bf(t), p["rhs"])) for t, p in zip(t_inv, pairs)]
    yield
    aw = [_dot(p["attn"], x) for p, x in zip(pairs, wu_bf)]
    kw = [_dot(p["k_tail_t"], x) for p, x in zip(pairs, wu_bf)]
    for p, aw_p, kw_p in zip(pairs, aw, kw):
        slot = p["d"] * n_chunks + p["n"]
        nq_rows = pl.multiple_of(slot * NQ_ROWS, NQ_ROWS)
        nq_ref[pl.ds(nq_rows, HEAD_DIM), :] = _bf(-kw_p[:, :LANES])
        nq_ref[pl.ds(nq_rows + HEAD_DIM, CHUNK), :] = _bf(p["q"] * p["e_g"] - aw_p[:, :LANES])
        b_ref[pl.ds(pl.multiple_of(slot * HEAD_DIM, HEAD_DIM), HEAD_DIM), :] = kw_p[:, LANES:]
        d_ref[pl.ds(slot, 1), :] = jnp.exp(p["g_edge"])
        o_ref[pl.ds(p["d"] * seq + p["row0"], CHUNK), :] = aw_p[:, LANES:]


def _gdn_step_actions(chunks, states, o_ref, nq_ref, b_ref, d_ref, n_chunks):
    seq = n_chunks * CHUNK

    def step(u):
        for d in range(2):
            n = chunks[d][u]
            slot = d * n_chunks + n
            nq = nq_ref[pl.ds(pl.multiple_of(slot * NQ_ROWS, NQ_ROWS), NQ_ROWS), :]
            res = _dot(nq, _bf(states[d]))
            o_ref[pl.ds((2 + d) * seq + pl.multiple_of(n * CHUNK, CHUNK), CHUNK), :] = res[HEAD_DIM:, :]
            b_mat = b_ref[pl.ds(pl.multiple_of(slot * HEAD_DIM, HEAD_DIM), HEAD_DIM), :]
            states[d] = states[d] * d_ref[pl.ds(slot, 1), :] + (res[:HEAD_DIM, :] + b_mat)

    return [functools.partial(step, u) for u in range(len(chunks[0]))]


def _mixers_kernel(gq_ref, gk_ref, gv_ref, ggate_ref, q_ref, k_ref, v_ref, z_ref, sm_ref,
                   dwf_ref, dbf_ref, dwb_ref, dbb_ref, gla_ng_ref, cq_ref, ck_ref, cv_ref, ng_ref,
                   y_gla_ref, y_ref,
                   lgf_ref, lgb_ref, go_ref,
                   qs_ref, ks_ref, vs_ref, gf_ref, gb_ref, bf_ref, bb_ref, o_ref, nq_ref, b_ref, d_ref):
    h = pl.program_id(1)
    seq = q_ref.shape[0]
    n_chunks = seq // CHUNK

    n_groups = n_chunks // MIX_GROUP
    slab_rows = MIX_GROUP * CHUNK

    def group(t):
        fwd = [t * MIX_GROUP + u for u in range(MIX_GROUP)]
        return fwd, [n_chunks - 1 - n for n in fwd]

    def gla_load(t):
        for lg_ref, dw_ref, db_ref, row0 in ((lgf_ref, dwf_ref, dbf_ref, t * slab_rows),
                                             (lgb_ref, dwb_ref, dbb_ref, seq - (t + 1) * slab_rows)):
            rows = pl.ds(pl.multiple_of(row0, slab_rows), slab_rows)
            pre = _dot(_bf(sm_ref[rows, :]), dw_ref[...]) + db_ref[...]
            lg_ref[rows, :] = -_softplus(-pre) / GLA_GATE_NORMALIZER

    r, c = _chunk_masks()
    lower, upper = r >= c, r <= c
    gla_dirs = ((lgf_ref, lower, lower.astype(F32).astype(BF16), CHUNK - 1),
                (lgb_ref, upper, upper.astype(F32).astype(BF16), 0))

    def gla_actions(t, states):
        gen = _gla_stages(group(t), states, gq_ref, gk_ref, gv_ref, gla_dirs, go_ref)
        return [functools.partial(next, gen, None) for _ in range(GLA_STAGES)]

    sel_row = lax.broadcasted_iota(jnp.int32, (LANES, 4 * LANES), 0)
    sel_grp = lax.broadcasted_iota(jnp.int32, (LANES, 4 * LANES), 1) // LANES
    sel = (sel_row == _SM_A_F + HEADS * sel_grp + h).astype(F32).astype(BF16)

    def l2_norm(t, scale=1.0):
        return t * (lax.rsqrt(jnp.sum(t * t, axis=-1, keepdims=True) + NORM_EPS) * scale)

    def load_slab(s):
        r0, r1 = s * slab_rows, (s + 1) * slab_rows

        def conv_silu(x_ref, c_ref):
            before = x_ref[r0 - 1:r0, :] if s > 0 else None
            after = x_ref[r1:r1 + 1, :] if r1 < seq else None
            return _silu_of_half(_conv3(x_ref[r0:r1, :], 0.5 * c_ref[...], before, after))

        qs_ref[r0:r1, :] = l2_norm(conv_silu(q_ref, cq_ref), HEAD_DIM ** -0.5)
        ks_ref[r0:r1, :] = l2_norm(conv_silu(k_ref, ck_ref))
        vs_ref[r0:r1, :] = conv_silu(v_ref, cv_ref)
        sm_hi, sm_lo = _split(sm_ref[r0:r1, :])
        cols = _dot(sm_hi, sel) + _dot(sm_lo, sel)
        for i, ref in enumerate((gf_ref, gb_ref, bf_ref, bb_ref)):
            ref[r0:r1, :] = cols[:, i * LANES:(i + 1) * LANES]

    eye = (r == c).astype(F32)
    dirs = ((gf_ref, bf_ref, lower, r > c, CHUNK - 1),
            (gb_ref, bb_ref, upper, r < c, 0))

    def prepare(t):
        return _gdn_prepare(group(t), qs_ref, ks_ref, vs_ref, dirs, o_ref, nq_ref, b_ref, d_ref,
                            r, c, eye, n_chunks)

    def woven(t, gdn_states, gla_states):
        steps = _gdn_step_actions(group(t), gdn_states, o_ref, nq_ref, b_ref, d_ref, n_chunks)
        stages = gla_actions(t, gla_states)

        def both(step, stage):
            step()
            stage()

        return [functools.partial(both, s, g) for s, g in zip(steps, stages)] + steps[len(stages):]

    def body(t, carry):
        gdn_states, gla_states = list(carry[:2]), list(carry[2:])
        acts = woven(t - 1, gdn_states, gla_states)
        acts.insert(1, functools.partial(gla_load, jnp.minimum(t + 1, n_groups - 1)))
        _interleave(prepare(t), acts)
        return tuple(gdn_states + gla_states)

    zero = jnp.zeros((HEAD_DIM, HEAD_DIM), F32)
    load_slab(0)
    load_slab(n_groups - 1)
    gla_load(0)
    _interleave(prepare(0), [functools.partial(load_slab, s) for s in range(1, n_groups - 1)]
                + [functools.partial(gla_load, 1)])
    carry = lax.fori_loop(1, n_groups, body, (zero,) * 4)
    gdn_states, gla_states = list(carry[:2]), list(carry[2:])
    for act in woven(n_groups - 1, gdn_states, gla_states):
        act()
    o = o_ref[0:seq, :] + o_ref[seq:2 * seq, :] + o_ref[2 * seq:3 * seq, :] + o_ref[3 * seq:4 * seq, :]
    y_ref[...] = _gated_head_norm(o, z_ref[...], ng_ref[...]).astype(y_ref.dtype)
    go = go_ref[0:seq, :] + go_ref[seq:2 * seq, :]
    y_gla_ref[...] = _gated_head_norm(go, ggate_ref[...], gla_ng_ref[...]).astype(y_gla_ref.dtype)


def _mixers(p, sm, dw_f, db_f, dw_b, db_b, gla_norm_g, conv_w, gdn_norm_g, batch, seq):
    def col(base):
        return pl.BlockSpec((seq, LANES), lambda b, h: (b, base + h))

    def conv_col(base):
        return pl.BlockSpec((3, LANES), lambda b, h: (0, base + h))

    head_vec = pl.BlockSpec((1, LANES), lambda b, h: (0, h))
    head_mat = pl.BlockSpec((LANES, LANES), lambda b, h: (0, h))
    shared_vec = pl.BlockSpec((1, LANES), lambda b, h: (0, 0))
    head_out = pl.BlockSpec((seq, LANES), lambda b, h: (b, h))
    slots = 2 * (seq // CHUNK)
    seq_buf = pltpu.VMEM((seq, LANES), F32)
    return pl.pallas_call(
        _mixers_kernel,
        grid=(batch, HEADS),
        in_specs=[col(_CB_GLA_Q), col(_CB_GLA_K), col(_CB_GLA_V), col(_CB_GLA_GATE),
                  col(_CB_GDN_Q), col(_CB_GDN_K), col(_CB_GDN_V), col(_CB_GDN_Z),
                  pl.BlockSpec((seq, LANES), lambda b, h: (b, 0)),
                  head_mat, head_vec, head_mat, head_vec, shared_vec,
                  conv_col(0), conv_col(HEADS), conv_col(2 * HEADS), shared_vec],
        out_specs=[head_out, head_out],
        out_shape=[jax.ShapeDtypeStruct((batch * seq, HEAD_W), BF16)] * 2,
        scratch_shapes=[
            seq_buf, seq_buf,
            pltpu.VMEM((2 * seq, LANES), F32),
            seq_buf, seq_buf, seq_buf,
            seq_buf, seq_buf, seq_buf, seq_buf,
            pltpu.VMEM((4 * seq, LANES), F32),
            pltpu.VMEM((slots * NQ_ROWS, LANES), BF16),
            pltpu.VMEM((slots * HEAD_DIM, LANES), F32),
            pltpu.VMEM((slots, LANES), F32)],
        compiler_params=pltpu.CompilerParams(
            dimension_semantics=("parallel", "arbitrary"), vmem_limit_bytes=VMEM_LIMIT),
        name="mixers",
    )(p, p, p, p, p, p, p, p, sm, dw_f, db_f, dw_b, db_b, gla_norm_g, conv_w, conv_w, conv_w, gdn_norm_g)


def _merge_out_kernel(yg_ref, yd_ref, gg_ref, gd_ref, x_ref, wbg_ref, wbd_ref, wo_ref, n2_ref,
                      x1_ref, h2_ref):
    merged = (_sigmoid(gg_ref[...]) * _dot(yg_ref[...], wbg_ref[...])
              + _sigmoid(gd_ref[...]) * _dot(yd_ref[...], wbd_ref[...]))
    x1 = x_ref[...] + _dot(merged.astype(BF16), wo_ref[...])
    x1_ref[...] = x1
    h2_ref[...] = (_rms_scale(x1) * n2_ref[...]).astype(h2_ref.dtype)


def _merge_out(y_gla, y_gdn, p, x2, w_bg, w_bd, w_o, n2, tm=256):
    m, d = x2.shape
    gate_cb = (8 * HEAD_W) // d

    def resident(shape):
        return pl.BlockSpec(shape, lambda i: (0, 0), pipeline_mode=pl.Buffered(1))

    return pl.pallas_call(
        _merge_out_kernel,
        grid=(m // tm,),
        in_specs=[
            pl.BlockSpec((tm, HEAD_W), lambda i: (i, 0)),
            pl.BlockSpec((tm, HEAD_W), lambda i: (i, 0)),
            pl.BlockSpec((tm, d), lambda i: (i, gate_cb)),
            pl.BlockSpec((tm, d), lambda i: (i, gate_cb + 1)),
            pl.BlockSpec((tm, d), lambda i: (i, 0)),
            resident((HEAD_W, d)), resident((HEAD_W, d)), resident((d, d)), resident((1, d)),
        ],
        out_specs=[pl.BlockSpec((tm, d), lambda i: (i, 0)), pl.BlockSpec((tm, d), lambda i: (i, 0))],
        out_shape=[jax.ShapeDtypeStruct((m, d), F32), jax.ShapeDtypeStruct((m, d), BF16)],
        compiler_params=pltpu.CompilerParams(
            dimension_semantics=("parallel",), vmem_limit_bytes=VMEM_LIMIT),
        name="merge_out",
    )(y_gla, y_gdn, p, p, x2, w_bg, w_bd, w_o, n2)


def _ffn_up_kernel(h_ref, wg_ref, wv_ref, cg_ref, cv_ref, bg_ref, bv_ref, act_ref, u_ref):
    seq = h_ref.shape[0]
    n_slabs = seq // FFN_ROWS
    lo, hi = SUBLANES, SUBLANES + FFN_ROWS
    params = ((_bf(wg_ref[...]), 0.5 * cg_ref[...], 0.5 * bg_ref[...]), (_bf(wv_ref[...]), cv_ref[...], bv_ref[...]))
    zero_row = jnp.zeros((1, act_ref.shape[1]), F32)

    def project(s):
        h = h_ref[s * FFN_ROWS:(s + 1) * FFN_ROWS, :]
        edges = []
        for i, (w_bf, _, _) in enumerate(params):
            u = _dot(h, w_bf)
            u_ref[s % 2, i, lo:hi, :] = u
            edges.append((u[:1, :], u[FFN_ROWS - 1:, :]))
        return edges

    def finish(s, before, after):
        out = []
        for i, (_, cw, b) in enumerate(params):
            u_ref[s % 2, i, lo - 1:lo, :] = zero_row if before is None else before[i][1]
            u_ref[s % 2, i, hi:hi + 1, :] = zero_row if after is None else after[i][0]
            out.append(u_ref[s % 2, i, lo - 1:hi - 1, :] * cw[0:1, :] + u_ref[s % 2, i, lo:hi, :] * cw[1:2, :]
                       + u_ref[s % 2, i, lo + 1:hi + 1, :] * cw[2:3, :] + b)
        act_ref[s * FFN_ROWS:(s + 1) * FFN_ROWS, :] = (_silu_of_half(out[0]) * out[1]).astype(act_ref.dtype)

    edges = [project(0)]
    for s in range(1, n_slabs):
        edges.append(project(s))
        finish(s - 1, edges[s - 2] if s >= 2 else None, edges[s])
    finish(n_slabs - 1, edges[-2] if n_slabs >= 2 else None, None)


def _ffn_up(h2, w_up, conv_w, conv_b, batch, seq, tn=256):
    d = h2.shape[1]
    nj = D_FF // tn
    return pl.pallas_call(
        _ffn_up_kernel,
        grid=(batch, nj),
        in_specs=[
            pl.BlockSpec((seq, d), lambda b, j: (b, 0)),
            pl.BlockSpec((d, tn), lambda b, j: (0, j)),
            pl.BlockSpec((d, tn), lambda b, j: (0, nj + j)),
            pl.BlockSpec((3, tn), lambda b, j: (0, j)),
            pl.BlockSpec((3, tn), lambda b, j: (0, nj + j)),
            pl.BlockSpec((1, tn), lambda b, j: (0, j)),
            pl.BlockSpec((1, tn), lambda b, j: (0, nj + j)),
        ],
        out_specs=pl.BlockSpec((seq, tn), lambda b, j: (b, j)),
        out_shape=jax.ShapeDtypeStruct((batch * seq, D_FF), BF16),
        scratch_shapes=[pltpu.VMEM((2, 2, FFN_ROWS + 2 * SUBLANES, tn), F32)],
        compiler_params=pltpu.CompilerParams(
            dimension_semantics=("parallel", "arbitrary"), vmem_limit_bytes=VMEM_LIMIT),
        name="ffn_up",
    )(h2, w_up, w_up, conv_w, conv_w, conv_b, conv_b)


def _ffn_down_kernel(act_ref, w_ref, x1_ref, g_ref, out_ref, *, final_norm):
    x2 = x1_ref[...] + _dot(act_ref[...], w_ref[...])
    out_ref[...] = _rms_scale(x2) * g_ref[...] if final_norm else x2


def _ffn_down(act, w_down, x1, g, final_norm, tm=256):
    m, d = x1.shape
    return pl.pallas_call(
        functools.partial(_ffn_down_kernel, final_norm=final_norm),
        grid=(m // tm,),
        in_specs=[
            pl.BlockSpec((tm, D_FF), lambda i: (i, 0)),
            pl.BlockSpec((D_FF, d), lambda i: (0, 0), pipeline_mode=pl.Buffered(1)),
            pl.BlockSpec((tm, d), lambda i: (i, 0)),
            pl.BlockSpec((1, d), lambda i: (0, 0)),
        ],
        out_specs=pl.BlockSpec((tm, d), lambda i: (i, 0)),
        out_shape=jax.ShapeDtypeStruct((m, d), F32),
        compiler_params=pltpu.CompilerParams(
            dimension_semantics=("parallel",), vmem_limit_bytes=VMEM_LIMIT),
        name="ffn_down",
    )(act, w_down, x1, g)


def _pad_rows(w, row0, rows):
    return jnp.zeros((rows, w.shape[1]), w.dtype).at[row0:row0 + w.shape[0]].set(w)


def kernel(x, norm1_g, w_in, gla_decay_w_f, gla_decay_b_f, gla_decay_w_b, gla_decay_b_b, gla_norm_g,
           gdn_conv_w, gdn_a_log_f, gdn_dt_bias_f, gdn_a_log_b, gdn_dt_bias_b, gdn_norm_g,
           w_branch_gla, w_branch_gdn, w_out, norm2_g, w_up, ffn_conv_w, ffn_conv_b, w_down,
           final_norm_g):
    batch, seq, d = x.shape
    depth = w_in.shape[0]
    x2 = x.reshape(batch * seq, d)
    for l in range(depth):
        wl_t = jnp.swapaxes(w_in, 1, 2)[l]
        w_big = _repack_w_in(wl_t, _OFF_GDN, 3, N_GROUP)
        w_small = jnp.concatenate(
            [wl_t[_OFF_LR:_OFF_GDN], wl_t[_OFF_AB:_OFF_GATES],
             jnp.zeros((LANES - 2 * GLA_LOW_RANK - 4 * HEADS, d), wl_t.dtype)], axis=0)
        pad = jnp.zeros((LANES - _SM_BETA_F,), F32)

        def gate_row(fwd, bwd):
            return jnp.concatenate([jnp.zeros((_SM_A_F,), F32), fwd, bwd, pad])[None, :]

        p, sm = _in_proj(x2, norm1_g[l][None, :], w_big, w_small,
                         gate_row(gdn_a_log_f[l], gdn_a_log_b[l]),
                         gate_row(gdn_dt_bias_f[l], gdn_dt_bias_b[l]))

        y_gla, y_gdn = _mixers(
            p, sm,
            _pad_rows(gla_decay_w_f[l], 0, LANES).astype(BF16), gla_decay_b_f[l][None, :],
            _pad_rows(gla_decay_w_b[l], GLA_LOW_RANK, LANES).astype(BF16), gla_decay_b_b[l][None, :],
            gla_norm_g[l][None, :], gdn_conv_w[l], gdn_norm_g[l][None, :], batch, seq)

        x2, h2 = _merge_out(y_gla, y_gdn, p, x2,
                            w_branch_gla[l].astype(BF16), w_branch_gdn[l].astype(BF16),
                            w_out[l].astype(BF16), norm2_g[l][None, :])
        act = _ffn_up(h2, w_up[l], ffn_conv_w[l], ffn_conv_b[l][None, :], batch, seq)
        x2 = _ffn_down(act, w_down[l].astype(BF16), x2, final_norm_g[None, :], l == depth - 1)
    return x2.reshape(batch, seq, d)
```

```python
import functools
import math

import jax
import jax.numpy as jnp
from jax import lax
from jax.experimental import pallas as pl
from jax.experimental.pallas import tpu as pltpu

F32 = jnp.float32
BF16 = jnp.bfloat16

D_MODEL = 2048
HEADS = 8
HEAD_DIM = 128
HEAD_W = HEADS * HEAD_DIM
GLA_LOW_RANK = 16
GLA_GATE_NORMALIZER = 16.0
D_FF = 5632
CHUNK = 64
NORM_EPS = 1e-6
LANES = 128
SUBLANES = 8
CUM_ROWS = 256
FFN_ROWS = 256
MIX_GROUP = 8
NQ_ROWS = HEAD_DIM + CHUNK

_OFF_LR = 4 * HEAD_W
_OFF_GDN = _OFF_LR + 2 * GLA_LOW_RANK
_OFF_AB = _OFF_GDN + 4 * HEAD_W
_OFF_GATES = _OFF_AB + 4 * HEADS
N_GROUP = 4 * HEAD_W
assert _OFF_LR == N_GROUP and _OFF_AB - _OFF_GDN == N_GROUP and 2 * D_MODEL == N_GROUP
assert _OFF_GATES == 2 * _OFF_GDN
_SM_A_F, _SM_A_B, _SM_BETA_F, _SM_BETA_B = 32, 40, 48, 56
_CB_GLA_Q, _CB_GLA_K, _CB_GLA_V, _CB_GLA_GATE = 0, 8, 16, 24
_CB_GDN_Q, _CB_GDN_K, _CB_GDN_V, _CB_GDN_Z = 32, 40, 48, 56

VMEM_LIMIT = 56 * 1024 * 1024


def _dot(a, b):
    return jnp.dot(a, b, preferred_element_type=F32)


def _dot_nt(a, b):
    return lax.dot_general(a, b, (((1,), (1,)), ((), ())), preferred_element_type=F32)


def _dot_tn(a, b):
    return lax.dot_general(a, b, (((0,), (0,)), ((), ())), preferred_element_type=F32)


def _bf(x):
    return x.astype(BF16)


def _split(x):
    hi = x.astype(BF16)
    lo = (x - hi.astype(F32)).astype(BF16)
    return hi, lo


def _dot_exact_rhs(a_bf, b):
    hi, lo = _split(b)
    return _dot(a_bf, hi) + _dot(a_bf, lo)


def _dot_exact_lhs(a, b_bf):
    hi, lo = _split(a)
    return _dot(hi, b_bf) + _dot(lo, b_bf)


def _dot3(a, b):
    ah, al = _split(a)
    bh, bl = _split(b)
    return _dot(ah, bh) + (_dot(ah, bl) + _dot(al, bh))


def _sigmoid(x):
    return 0.5 * jnp.tanh(0.5 * x) + 0.5


def _silu_of_half(h):
    return h * jnp.tanh(h) + h


def _silu(x):
    return _silu_of_half(0.5 * x)


def _softplus(x):
    return jnp.maximum(x, 0.0) + jnp.log1p(jnp.exp(-jnp.abs(x)))


def _conv3(x, cw, row_before=None, row_after=None):
    n = x.shape[0]
    sub = lax.broadcasted_iota(jnp.int32, (SUBLANES, 1), 0)
    zero = jnp.zeros((1, x.shape[1]), x.dtype)
    x_prev = pltpu.roll(x, 1, 0)
    first = jnp.where(sub == 0, zero if row_before is None else row_before, x_prev[:SUBLANES, :])
    x_prev = jnp.concatenate([first, x_prev[SUBLANES:, :]], axis=0)
    x_next = pltpu.roll(x, n - 1, 0)
    last = jnp.where(sub == SUBLANES - 1, zero if row_after is None else row_after, x_next[n - SUBLANES:, :])
    x_next = jnp.concatenate([x_next[:n - SUBLANES, :], last], axis=0)
    return x_prev * cw[0:1, :] + x * cw[1:2, :] + x_next * cw[2:3, :]


def _rms_scale(x):
    return x * lax.rsqrt(jnp.mean(x * x, axis=-1, keepdims=True) + NORM_EPS)


def _repack_kernel(w_ref, o_ref):
    o_ref[...] = w_ref[...].astype(o_ref.dtype)


def _repack_w_in(w_t, group_stride, groups, width, tr=512):
    d = w_t.shape[1]
    return pl.pallas_call(
        _repack_kernel,
        grid=(groups, width // tr),
        in_specs=[pl.BlockSpec(
            (pl.Element(tr), pl.Element(d)),
            lambda g, i: (pl.multiple_of(g * group_stride + i * tr, math.gcd(group_stride, tr)), 0))],
        out_specs=pl.BlockSpec((None, tr, d), lambda g, i: (g, i, 0)),
        out_shape=jax.ShapeDtypeStruct((groups, width, d), BF16),
        compiler_params=pltpu.CompilerParams(
            dimension_semantics=("parallel", "parallel"), vmem_limit_bytes=VMEM_LIMIT),
        name="repack_w_in",
    )(w_t)


def _gdn_gates(sm, alog_row, dtb_row):
    tm = sm.shape[0]
    col = lax.broadcasted_iota(jnp.int32, (1, LANES), 1)
    is_g = (col >= _SM_A_F) & (col < _SM_BETA_F)
    is_fwd = col < _SM_A_B
    is_beta = (col >= _SM_BETA_F) & (col < _SM_BETA_B + HEADS)
    g = jnp.where(is_g, -jnp.exp(alog_row) * _softplus(sm + dtb_row), 0.0)
    r = lax.broadcasted_iota(jnp.int32, (CUM_ROWS, CUM_ROWS), 0)
    c = lax.broadcasted_iota(jnp.int32, (CUM_ROWS, CUM_ROWS), 1)
    same_chunk = (r // CHUNK) == (c // CHUNK)
    tril_bf = (same_chunk & (r >= c)).astype(F32).astype(BF16)
    triu_bf = (same_chunk & (r <= c)).astype(F32).astype(BF16)
    slabs = []
    for s in range(tm // CUM_ROWS):
        hi, lo = _split(g[s * CUM_ROWS:(s + 1) * CUM_ROWS])
        prefix = _dot(tril_bf, hi) + _dot(tril_bf, lo)
        suffix = _dot(triu_bf, hi) + _dot(triu_bf, lo)
        slabs.append(jnp.where(is_fwd, prefix, suffix))
    g_cum = jnp.concatenate(slabs, axis=0)
    return jnp.where(is_g, g_cum, jnp.where(is_beta, _sigmoid(sm), sm))


def _in_proj_kernel(x_ref, g_ref, w_ref, ws_ref, alog_ref, dtb_ref, p_ref, sm_ref, hn_ref):
    @pl.when(pl.program_id(1) == 0)
    def _():
        hn = (_rms_scale(x_ref[...]) * g_ref[...]).astype(BF16)
        hn_ref[...] = hn
        sm_ref[...] = _gdn_gates(_dot_nt(hn, _bf(ws_ref[...])), alog_ref[...], dtb_ref[...])

    res = _dot_nt(hn_ref[...], w_ref[...])
    for cb in range(p_ref.shape[0]):
        p_ref[cb] = res[:, cb * LANES:(cb + 1) * LANES]


def _in_proj(x2, g, w_big, w_small, alog_row, dtb_row, tm=1024, tn=1024):
    m, d = x2.shape
    groups, n_group, _ = w_big.shape
    per_group = n_group // tn
    n = groups * n_group
    return pl.pallas_call(
        _in_proj_kernel,
        grid=(m // tm, n // tn),
        in_specs=[
            pl.BlockSpec((tm, d), lambda i, j: (i, 0)),
            pl.BlockSpec((1, d), lambda i, j: (0, 0)),
            pl.BlockSpec((None, tn, d), lambda i, j: (j // per_group, j % per_group, 0)),
            pl.BlockSpec((LANES, d), lambda i, j: (0, 0)),
            pl.BlockSpec((1, LANES), lambda i, j: (0, 0)),
            pl.BlockSpec((1, LANES), lambda i, j: (0, 0)),
        ],
        out_specs=[
            pl.BlockSpec((tn // LANES, tm, LANES), lambda i, j: (j, i, 0)),
            pl.BlockSpec((tm, LANES), lambda i, j: (i, 0)),
        ],
        out_shape=[
            jax.ShapeDtypeStruct((n // LANES, m, LANES), F32),
            jax.ShapeDtypeStruct((m, LANES), F32),
        ],
        scratch_shapes=[pltpu.VMEM((tm, d), BF16)],
        compiler_params=pltpu.CompilerParams(
            dimension_semantics=("parallel", "arbitrary"), vmem_limit_bytes=VMEM_LIMIT),
        name="in_proj",
    )(x2, g, w_big, w_small, alog_row, dtb_row)


def _chunk_masks():
    r = lax.broadcasted_iota(jnp.int32, (CHUNK, CHUNK), 0)
    c = lax.broadcasted_iota(jnp.int32, (CHUNK, CHUNK), 1)
    return r, c


def _gated_head_norm(o, z, g):
    return (_rms_scale(o) * g) * _silu(z)


def _gla_stages(chunks, states, q_ref, k_ref, v_ref, dirs, o_ref):
    pairs = []
    for u in range(len(chunks[0])):
        for d, (lg_ref, incl, tri_bf, edge) in enumerate(dirs):
            row0 = pl.multiple_of(chunks[d][u] * CHUNK, CHUNK)
            rows = pl.ds(row0, CHUNK)
            hi, lo = _split(lg_ref[rows, :])
            pairs.append(dict(d=d, row0=row0, rows=rows, incl=incl, edge=edge, hi=hi, lo=lo, tri=tri_bf))
    for p in pairs:
        p["g_cum"] = _dot(p["tri"], p["hi"]) + _dot(p["tri"], p["lo"])
    yield
    for p in pairs:
        g_cum = p["g_cum"]
        g_edge = g_cum[p["edge"]:p["edge"] + 1, :]
        k = k_ref[p["rows"], :]
        p["v_bf"] = _bf(v_ref[p["rows"], :])
        p["q_dec"] = _bf(q_ref[p["rows"], :] * (HEAD_DIM ** -0.5) * jnp.exp(g_cum))
        p["decay"] = jnp.exp(g_edge)
        p["scores"] = _dot_nt(p["q_dec"], _bf(k * jnp.exp(-g_cum)))
        p["upd"] = _dot_tn(p["v_bf"], _bf(k * jnp.exp(g_edge - g_cum)))
    yield
    for p in pairs:
        p["o"] = _dot(_bf(jnp.where(p["incl"], p["scores"], 0.0)), p["v_bf"])
    yield
    seq = q_ref.shape[0]
    for p in pairs:
        st = states[p["d"]]
        o_ref[pl.ds(p["d"] * seq + p["row0"], CHUNK), :] = p["o"] + _dot_nt(p["q_dec"], _bf(st))
        states[p["d"]] = st * p["decay"] + p["upd"]


GLA_STAGES = 4


def _unit_tri_inverses(l_mats, r, c, eye):
    blk16 = (r // 16) == (c // 16)
    blk32 = (r // 32) == (c // 32)
    a = [-jnp.where(blk16, l, 0.0) for l in l_mats]
    a_bf = [_bf(x) for x in a]
    p2_bf = [_bf(_dot(x, x)) for x in a_bf]
    yield
    t = [eye + x for x in a]
    t = [x + _dot(_bf(x), p) for x, p in zip(t, p2_bf)]
    p4_bf = [_bf(_dot(p, p)) for p in p2_bf]
    yield
    t = [x + _dot(_bf(x), p) for x, p in zip(t, p4_bf)]
    p8_bf = [_bf(_dot(p, p)) for p in p4_bf]
    yield
    t = [x + _dot(_bf(x), p) for x, p in zip(t, p8_bf)]
    yield
    for off_diag in (blk32 & ~blk16, ~blk32):
        t_bf = [_bf(x) for x in t]
        e_t = [_bf(_dot(_bf(jnp.where(off_diag, l, 0.0)), x)) for l, x in zip(l_mats, t_bf)]
        yield
        t = [x - _dot(xb, e) for x, xb, e in zip(t, t_bf, e_t)]
        yield
    return t


def _interleave(stages, actions):
    actions = list(actions)
    for _ in stages:
        if actions:
            actions.pop(0)()
    for act in actions:
        act()


def _gdn_prepare(chunks, qs_ref, ks_ref, vs_ref, dirs, o_ref, nq_ref, b_ref, d_ref, r, c, eye, n_chunks):
    seq = n_chunks * CHUNK
    pairs = []
    for u in range(len(chunks[0])):
        for d, (g_ref, beta_ref, incl, strict, edge) in enumerate(dirs):
            n = chunks[d][u]
            row0 = pl.multiple_of(n * CHUNK, CHUNK)
            rows = pl.ds(row0, CHUNK)
            pairs.append(dict(n=n, d=d, row0=row0, q=qs_ref[rows, :], k=ks_ref[rows, :], v=vs_ref[rows, :],
                              g_cum=g_ref[rows, :], beta=beta_ref[rows, :],
                              incl=incl, strict=strict, edge=edge))
    for p in pairs:
        k_bf = _bf(p["k"])
        p["kk"] = _dot_nt(k_bf, k_bf)
        p["qk"] = _dot_nt(_bf(p["q"]), k_bf)
    yield
    for p in pairs:
        g_cum, beta = p["g_cum"], p["beta"]
        g_cum_t = g_cum.T
        diff = g_cum[:, :CHUNK] - g_cum_t[:CHUNK, :]
        decay = jnp.where(p["incl"], jnp.exp(jnp.where(p["incl"], diff, 0.0)), 0.0)
        p["l_mat"] = jnp.where(p["strict"], p["kk"], 0.0) * decay * beta[:, :CHUNK]
        p["e_g"] = jnp.exp(g_cum)
        p["g_edge"] = g_cum[p["edge"]:p["edge"] + 1, :]
        p["rhs"] = _bf(jnp.concatenate([p["k"] * (beta * p["e_g"]), p["v"] * beta], axis=1))
        p["attn"] = _bf(p["qk"] * decay)
        p["k_tail"] = _bf(p["k"] * jnp.exp(p["g_edge"] - g_cum))
    t_inv = yield from _unit_tri_inverses([p["l_mat"] for p in pairs], r, c, eye)
    wu_bf = [_bf(_dot(_bf(t), p["rhs"])) for t, p in zip(t_inv, pairs)]
    yield
    aw = [_dot(p["attn"], x) for p, x in zip(pairs, wu_bf)]
    kw = [_dot_tn(p["k_tail"], x) for p, x in zip(pairs, wu_bf)]
    for p, aw_p, kw_p in zip(pairs, aw, kw):
        slot = p["d"] * n_chunks + p["n"]
        nq_rows = pl.multiple_of(slot * NQ_ROWS, NQ_ROWS)
        nq_ref[pl.ds(nq_rows, HEAD_DIM), :] = _bf(-kw_p[:, :LANES])
        nq_ref[pl.ds(nq_rows + HEAD_DIM, CHUNK), :] = _bf(p["q"] * p["e_g"] - aw_p[:, :LANES])
        b_ref[pl.ds(pl.multiple_of(slot * HEAD_DIM, HEAD_DIM), HEAD_DIM), :] = kw_p[:, LANES:]
        d_ref[pl.ds(slot, 1), :] = jnp.exp(p["g_edge"])
        o_ref[pl.ds(p["d"] * seq + p["row0"], CHUNK), :] = aw_p[:, LANES:]


def _gdn_step_actions(chunks, states, o_ref, nq_ref, b_ref, d_ref, n_chunks):
    seq = n_chunks * CHUNK

    def step(u):
        for d in range(2):
            n = chunks[d][u]
            slot = d * n_chunks + n
            nq = nq_ref[pl.ds(pl.multiple_of(slot * NQ_ROWS, NQ_ROWS), NQ_ROWS), :]
            res = _dot(nq, _bf(states[d]))
            o_ref[pl.ds((2 + d) * seq + pl.multiple_of(n * CHUNK, CHUNK), CHUNK), :] = res[HEAD_DIM:, :]
            b_mat = b_ref[pl.ds(pl.multiple_of(slot * HEAD_DIM, HEAD_DIM), HEAD_DIM), :]
            states[d] = states[d] * d_ref[pl.ds(slot, 1), :] + (res[:HEAD_DIM, :] + b_mat)

    return [functools.partial(step, u) for u in range(len(chunks[0]))]


def _mixers_kernel(gq_ref, gk_ref, gv_ref, ggate_ref, q_ref, k_ref, v_ref, z_ref, sm_ref,
                   dwf_ref, dbf_ref, dwb_ref, dbb_ref, gla_ng_ref, cq_ref, ck_ref, cv_ref, ng_ref,
                   y_gla_ref, y_ref,
                   lgf_ref, lgb_ref, go_ref,
                   qs_ref, ks_ref, vs_ref, gf_ref, gb_ref, bf_ref, bb_ref, o_ref, nq_ref, b_ref, d_ref):
    h = pl.program_id(1)
    seq = q_ref.shape[0]
    n_chunks = seq // CHUNK

    n_groups = n_chunks // MIX_GROUP
    slab_rows = MIX_GROUP * CHUNK

    def group(t):
        fwd = [t * MIX_GROUP + u for u in range(MIX_GROUP)]
        return fwd, [n_chunks - 1 - n for n in fwd]

    def gla_load(t):
        for lg_ref, dw_ref, db_ref, row0 in ((lgf_ref, dwf_ref, dbf_ref, t * slab_rows),
                                             (lgb_ref, dwb_ref, dbb_ref, seq - (t + 1) * slab_rows)):
            rows = pl.ds(pl.multiple_of(row0, slab_rows), slab_rows)
            pre = _dot(_bf(sm_ref[rows, :]), dw_ref[...]) + db_ref[...]
            lg_ref[rows, :] = -_softplus(-pre) / GLA_GATE_NORMALIZER

    r, c = _chunk_masks()
    lower, upper = r >= c, r <= c
    gla_dirs = ((lgf_ref, lower, lower.astype(F32).astype(BF16), CHUNK - 1),
                (lgb_ref, upper, upper.astype(F32).astype(BF16), 0))

    def gla_actions(t, states):
        gen = _gla_stages(group(t), states, gq_ref, gk_ref, gv_ref, gla_dirs, go_ref)
        return [functools.partial(next, gen, None) for _ in range(GLA_STAGES)]

    sel_row = lax.broadcasted_iota(jnp.int32, (LANES, 4 * LANES), 0)
    sel_grp = lax.broadcasted_iota(jnp.int32, (LANES, 4 * LANES), 1) // LANES
    sel = (sel_row == _SM_A_F + HEADS * sel_grp + h).astype(F32).astype(BF16)

    def l2_norm(t, scale=1.0):
        return t * (lax.rsqrt(jnp.sum(t * t, axis=-1, keepdims=True) + NORM_EPS) * scale)

    def load_slab(s):
        r0, r1 = s * slab_rows, (s + 1) * slab_rows

        def conv_silu(x_ref, c_ref):
            before = x_ref[r0 - 1:r0, :] if s > 0 else None
            after = x_ref[r1:r1 + 1, :] if r1 < seq else None
            return _silu_of_half(_conv3(x_ref[r0:r1, :], 0.5 * c_ref[...], before, after))

        qs_ref[r0:r1, :] = l2_norm(conv_silu(q_ref, cq_ref), HEAD_DIM ** -0.5)
        ks_ref[r0:r1, :] = l2_norm(conv_silu(k_ref, ck_ref))
        vs_ref[r0:r1, :] = conv_silu(v_ref, cv_ref)
        sm_hi, sm_lo = _split(sm_ref[r0:r1, :])
        cols = _dot(sm_hi, sel) + _dot(sm_lo, sel)
        for i, ref in enumerate((gf_ref, gb_ref, bf_ref, bb_ref)):
            ref[r0:r1, :] = cols[:, i * LANES:(i + 1) * LANES]

    eye = (r == c).astype(F32)
    dirs = ((gf_ref, bf_ref, lower, r > c, CHUNK - 1),
            (gb_ref, bb_ref, upper, r < c, 0))

    def prepare(t):
        return _gdn_prepare(group(t), qs_ref, ks_ref, vs_ref, dirs, o_ref, nq_ref, b_ref, d_ref,
                            r, c, eye, n_chunks)

    def woven(t, gdn_states, gla_states):
        steps = _gdn_step_actions(group(t), gdn_states, o_ref, nq_ref, b_ref, d_ref, n_chunks)
        stages = gla_actions(t, gla_states)

        def both(step, stage):
            step()
            stage()

        return [functools.partial(both, s, g) for s, g in zip(steps, stages)] + steps[len(stages):]

    def body(t, carry):
        gdn_states, gla_states = list(carry[:2]), list(carry[2:])
        acts = woven(t - 1, gdn_states, gla_states)
        acts.insert(1, functools.partial(gla_load, jnp.minimum(t + 1, n_groups - 1)))
        _interleave(prepare(t), acts)
        return tuple(gdn_states + gla_states)

    zero = jnp.zeros((HEAD_DIM, HEAD_DIM), F32)
    load_slab(0)
    load_slab(n_groups - 1)
    gla_load(0)
    _interleave(prepare(0), [functools.partial(load_slab, s) for s in range(1, n_groups - 1)]
                + [functools.partial(gla_load, 1)])
    carry = lax.fori_loop(1, n_groups, body, (zero,) * 4)
    gdn_states, gla_states = list(carry[:2]), list(carry[2:])
    for act in woven(n_groups - 1, gdn_states, gla_states):
        act()
    o = o_ref[0:seq, :] + o_ref[seq:2 * seq, :] + o_ref[2 * seq:3 * seq, :] + o_ref[3 * seq:4 * seq, :]
    y_ref[...] = _gated_head_norm(o, z_ref[...], ng_ref[...]).astype(y_ref.dtype)
    go = go_ref[0:seq, :] + go_ref[seq:2 * seq, :]
    y_gla_ref[...] = _gated_head_norm(go, ggate_ref[...], gla_ng_ref[...]).astype(y_gla_ref.dtype)


def _mixers(p, sm, dw_f, db_f, dw_b, db_b, gla_norm_g, conv_w, gdn_norm_g, batch, seq):
    def col(base):
        return pl.BlockSpec((None, seq, LANES), lambda b, h: (base + h, b, 0))

    def conv_col(base):
        return pl.BlockSpec((3, LANES), lambda b, h: (0, base + h))

    head_vec = pl.BlockSpec((1, LANES), lambda b, h: (0, h))
    head_mat = pl.BlockSpec((LANES, LANES), lambda b, h: (0, h))
    shared_vec = pl.BlockSpec((1, LANES), lambda b, h: (0, 0))
    head_out = pl.BlockSpec((seq, LANES), lambda b, h: (b, h))
    slots = 2 * (seq // CHUNK)
    seq_buf = pltpu.VMEM((seq, LANES), F32)
    return pl.pallas_call(
        _mixers_kernel,
        grid=(batch, HEADS),
        in_specs=[col(_CB_GLA_Q), col(_CB_GLA_K), col(_CB_GLA_V), col(_CB_GLA_GATE),
                  col(_CB_GDN_Q), col(_CB_GDN_K), col(_CB_GDN_V), col(_CB_GDN_Z),
                  pl.BlockSpec((seq, LANES), lambda b, h: (b, 0)),
                  head_mat, head_vec, head_mat, head_vec, shared_vec,
                  conv_col(0), conv_col(HEADS), conv_col(2 * HEADS), shared_vec],
        out_specs=[head_out, head_out],
        out_shape=[jax.ShapeDtypeStruct((batch * seq, HEAD_W), BF16)] * 2,
        scratch_shapes=[
            seq_buf, seq_buf,
            pltpu.VMEM((2 * seq, LANES), F32),
            seq_buf, seq_buf, seq_buf,
            seq_buf, seq_buf, seq_buf, seq_buf,
            pltpu.VMEM((4 * seq, LANES), F32),
            pltpu.VMEM((slots * NQ_ROWS, LANES), BF16),
            pltpu.VMEM((slots * HEAD_DIM, LANES), F32),
            pltpu.VMEM((slots, LANES), F32)],
        compiler_params=pltpu.CompilerParams(
            dimension_semantics=("parallel", "arbitrary"), vmem_limit_bytes=VMEM_LIMIT),
        name="mixers",
    )(p, p, p, p, p, p, p, p, sm, dw_f, db_f, dw_b, db_b, gla_norm_g, conv_w, conv_w, conv_w, gdn_norm_g)


def _merge_out_kernel(yg_ref, yd_ref, gg_ref, gd_ref, x_ref, wbg_ref, wbd_ref, wo_ref, n2_ref,
                      x1_ref, h2_ref):
    branch_g = _dot(yg_ref[...], wbg_ref[...])
    branch_d = _dot(yd_ref[...], wbd_ref[...])
    merged = jnp.concatenate(
        [_sigmoid(gg_ref[cb]) * branch_g[:, cb * LANES:(cb + 1) * LANES]
         + _sigmoid(gd_ref[cb]) * branch_d[:, cb * LANES:(cb + 1) * LANES]
         for cb in range(gg_ref.shape[0])], axis=1)
    x1 = x_ref[...] + _dot(merged.astype(BF16), wo_ref[...])
    x1_ref[...] = x1
    h2_ref[...] = (_rms_scale(x1) * n2_ref[...]).astype(h2_ref.dtype)


def _merge_out(y_gla, y_gdn, p, x2, w_bg, w_bd, w_o, n2, tm=256):
    m, d = x2.shape
    gate_cb = (8 * HEAD_W) // d

    def resident(shape):
        return pl.BlockSpec(shape, lambda i: (0, 0), pipeline_mode=pl.Buffered(1))

    return pl.pallas_call(
        _merge_out_kernel,
        grid=(m // tm,),
        in_specs=[
            pl.BlockSpec((tm, HEAD_W), lambda i: (i, 0)),
            pl.BlockSpec((tm, HEAD_W), lambda i: (i, 0)),
            pl.BlockSpec((d // LANES, tm, LANES), lambda i: (gate_cb, i, 0)),
            pl.BlockSpec((d // LANES, tm, LANES), lambda i: (gate_cb + 1, i, 0)),
            pl.BlockSpec((tm, d), lambda i: (i, 0)),
            resident((HEAD_W, d)), resident((HEAD_W, d)), resident((d, d)), resident((1, d)),
        ],
        out_specs=[pl.BlockSpec((tm, d), lambda i: (i, 0)), pl.BlockSpec((tm, d), lambda i: (i, 0))],
        out_shape=[jax.ShapeDtypeStruct((m, d), F32), jax.ShapeDtypeStruct((m, d), BF16)],
        compiler_params=pltpu.CompilerParams(
            dimension_semantics=("parallel",), vmem_limit_bytes=VMEM_LIMIT),
        name="merge_out",
    )(y_gla, y_gdn, p, p, x2, w_bg, w_bd, w_o, n2)


def _ffn_up_kernel(h_ref, wg_ref, wv_ref, cg_ref, cv_ref, bg_ref, bv_ref, act_ref, u_ref):
    seq = h_ref.shape[0]
    n_slabs = seq // FFN_ROWS
    lo, hi = SUBLANES, SUBLANES + FFN_ROWS
    params = ((_bf(wg_ref[...]), 0.5 * cg_ref[...], 0.5 * bg_ref[...]), (_bf(wv_ref[...]), cv_ref[...], bv_ref[...]))
    zero_row = jnp.zeros((1, act_ref.shape[1]), F32)

    def project(s):
        h = h_ref[s * FFN_ROWS:(s + 1) * FFN_ROWS, :]
        edges = []
        for i, (w_bf, _, _) in enumerate(params):
            u = _dot(h, w_bf)
            u_ref[s % 2, i, lo:hi, :] = u
            edges.append((u[:1, :], u[FFN_ROWS - 1:, :]))
        return edges

    def finish(s, before, after):
        out = []
        for i, (_, cw, b) in enumerate(params):
            u_ref[s % 2, i, lo - 1:lo, :] = zero_row if before is None else before[i][1]
            u_ref[s % 2, i, hi:hi + 1, :] = zero_row if after is None else after[i][0]
            out.append(u_ref[s % 2, i, lo - 1:hi - 1, :] * cw[0:1, :] + u_ref[s % 2, i, lo:hi, :] * cw[1:2, :]
                       + u_ref[s % 2, i, lo + 1:hi + 1, :] * cw[2:3, :] + b)
        act_ref[s * FFN_ROWS:(s + 1) * FFN_ROWS, :] = (_silu_of_half(out[0]) * out[1]).astype(act_ref.dtype)

    edges = [project(0)]
    for s in range(1, n_slabs):
        edges.append(project(s))
        finish(s - 1, edges[s - 2] if s >= 2 else None, edges[s])
    finish(n_slabs - 1, edges[-2] if n_slabs >= 2 else None, None)


def _ffn_up(h2, w_up, conv_w, conv_b, batch, seq, tn=256):
    d = h2.shape[1]
    nj = D_FF // tn
    return pl.pallas_call(
        _ffn_up_kernel,
        grid=(batch, nj),
        in_specs=[
            pl.BlockSpec((seq, d), lambda b, j: (b, 0)),
            pl.BlockSpec((d, tn), lambda b, j: (0, j)),
            pl.BlockSpec((d, tn), lambda b, j: (0, nj + j)),
            pl.BlockSpec((3, tn), lambda b, j: (0, j)),
            pl.BlockSpec((3, tn), lambda b, j: (0, nj + j)),
            pl.BlockSpec((1, tn), lambda b, j: (0, j)),
            pl.BlockSpec((1, tn), lambda b, j: (0, nj + j)),
        ],
        out_specs=pl.BlockSpec((seq, tn), lambda b, j: (b, j)),
        out_shape=jax.ShapeDtypeStruct((batch * seq, D_FF), BF16),
        scratch_shapes=[pltpu.VMEM((2, 2, FFN_ROWS + 2 * SUBLANES, tn), F32)],
        compiler_params=pltpu.CompilerParams(
            dimension_semantics=("parallel", "arbitrary"), vmem_limit_bytes=VMEM_LIMIT),
        name="ffn_up",
    )(h2, w_up, w_up, conv_w, conv_w, conv_b, conv_b)


def _ffn_down_kernel(act_ref, w_ref, x1_ref, g_ref, out_ref, *, final_norm):
    x2 = x1_ref[...] + _dot(act_ref[...], w_ref[...])
    out_ref[...] = _rms_scale(x2) * g_ref[...] if final_norm else x2


def _ffn_down(act, w_down, x1, g, final_norm, tm=256):
    m, d = x1.shape
    return pl.pallas_call(
        functools.partial(_ffn_down_kernel, final_norm=final_norm),
        grid=(m // tm,),
        in_specs=[
            pl.BlockSpec((tm, D_FF), lambda i: (i, 0)),
            pl.BlockSpec((D_FF, d), lambda i: (0, 0), pipeline_mode=pl.Buffered(1)),
            pl.BlockSpec((tm, d), lambda i: (i, 0)),
            pl.BlockSpec((1, d), lambda i: (0, 0)),
        ],
        out_specs=pl.BlockSpec((tm, d), lambda i: (i, 0)),
        out_shape=jax.ShapeDtypeStruct((m, d), F32),
        compiler_params=pltpu.CompilerParams(
            dimension_semantics=("parallel",), vmem_limit_bytes=VMEM_LIMIT),
        name="ffn_down",
    )(act, w_down, x1, g)


def _pad_rows(w, row0, rows):
    return jnp.zeros((rows, w.shape[1]), w.dtype).at[row0:row0 + w.shape[0]].set(w)


def kernel(x, norm1_g, w_in, gla_decay_w_f, gla_decay_b_f, gla_decay_w_b, gla_decay_b_b, gla_norm_g,
           gdn_conv_w, gdn_a_log_f, gdn_dt_bias_f, gdn_a_log_b, gdn_dt_bias_b, gdn_norm_g,
           w_branch_gla, w_branch_gdn, w_out, norm2_g, w_up, ffn_conv_w, ffn_conv_b, w_down,
           final_norm_g):
    batch, seq, d = x.shape
    depth = w_in.shape[0]
    x2 = x.reshape(batch * seq, d)
    for l in range(depth):
        wl_t = jnp.swapaxes(w_in, 1, 2)[l]
        w_big = _repack_w_in(wl_t, _OFF_GDN, 3, N_GROUP)
        w_small = jnp.concatenate(
            [wl_t[_OFF_LR:_OFF_GDN], wl_t[_OFF_AB:_OFF_GATES],
             jnp.zeros((LANES - 2 * GLA_LOW_RANK - 4 * HEADS, d), wl_t.dtype)], axis=0)
        pad = jnp.zeros((LANES - _SM_BETA_F,), F32)

        def gate_row(fwd, bwd):
            return jnp.concatenate([jnp.zeros((_SM_A_F,), F32), fwd, bwd, pad])[None, :]

        p, sm = _in_proj(x2, norm1_g[l][None, :], w_big, w_small,
                         gate_row(gdn_a_log_f[l], gdn_a_log_b[l]),
                         gate_row(gdn_dt_bias_f[l], gdn_dt_bias_b[l]))

        y_gla, y_gdn = _mixers(
            p, sm,
            _pad_rows(gla_decay_w_f[l], 0, LANES).astype(BF16), gla_decay_b_f[l][None, :],
            _pad_rows(gla_decay_w_b[l], GLA_LOW_RANK, LANES).astype(BF16), gla_decay_b_b[l][None, :],
            gla_norm_g[l][None, :], gdn_conv_w[l], gdn_norm_g[l][None, :], batch, seq)

        x2, h2 = _merge_out(y_gla, y_gdn, p, x2,
                            w_branch_gla[l].astype(BF16), w_branch_gdn[l].astype(BF16),
                            w_out[l].astype(BF16), norm2_g[l][None, :])
        act = _ffn_up(h2, w_up[l], ffn_conv_w[l], ffn_conv_b[l][None, :], batch, seq)
        x2 = _ffn_down(act, w_down[l].astype(BF16), x2, final_norm_g[None, :], l == depth - 1)
    return x2.reshape(batch, seq, d)
```

```python
import functools
import math

import jax
import jax.numpy as jnp
from jax import lax
from jax.experimental import pallas as pl
from jax.experimental.pallas import tpu as pltpu

F32 = jnp.float32
BF16 = jnp.bfloat16

D_MODEL = 2048
HEADS = 8
HEAD_DIM = 128
HEAD_W = HEADS * HEAD_DIM
GLA_LOW_RANK = 16
GLA_GATE_NORMALIZER = 16.0
D_FF = 5632
CHUNK = 64
NORM_EPS = 1e-6
LANES = 128
SUBLANES = 8
CUM_ROWS = 256
FFN_ROWS = 256
MIX_GROUP = 8
NQ_ROWS = HEAD_DIM + CHUNK

_OFF_LR = 4 * HEAD_W
_OFF_GDN = _OFF_LR + 2 * GLA_LOW_RANK
_OFF_AB = _OFF_GDN + 4 * HEAD_W
_OFF_GATES = _OFF_AB + 4 * HEADS
N_GROUP = 4 * HEAD_W
assert _OFF_LR == N_GROUP and _OFF_AB - _OFF_GDN == N_GROUP and 2 * D_MODEL == N_GROUP
assert _OFF_GATES == 2 * _OFF_GDN
_SM_A_F, _SM_A_B, _SM_BETA_F, _SM_BETA_B = 32, 40, 48, 56
_CB_GLA_Q, _CB_GLA_K, _CB_GLA_V, _CB_GLA_GATE = 0, 8, 16, 24
_CB_GDN_Q, _CB_GDN_K, _CB_GDN_V, _CB_GDN_Z = 32, 40, 48, 56

VMEM_LIMIT = 56 * 1024 * 1024


def _dot(a, b):
    return jnp.dot(a, b, preferred_element_type=F32)


def _dot_nt(a, b):
    return lax.dot_general(a, b, (((1,), (1,)), ((), ())), preferred_element_type=F32)


def _dot_tn(a, b):
    return lax.dot_general(a, b, (((0,), (0,)), ((), ())), preferred_element_type=F32)


def _bf(x):
    return x.astype(BF16)


def _split(x):
    hi = x.astype(BF16)
    lo = (x - hi.astype(F32)).astype(BF16)
    return hi, lo


def _dot_exact_rhs(a_bf, b):
    hi, lo = _split(b)
    return _dot(a_bf, hi) + _dot(a_bf, lo)


def _dot_exact_lhs(a, b_bf):
    hi, lo = _split(a)
    return _dot(hi, b_bf) + _dot(lo, b_bf)


def _dot3(a, b):
    ah, al = _split(a)
    bh, bl = _split(b)
    return _dot(ah, bh) + (_dot(ah, bl) + _dot(al, bh))


def _sigmoid(x):
    return 0.5 * jnp.tanh(0.5 * x) + 0.5


def _silu_of_half(h):
    return h * jnp.tanh(h) + h


def _silu(x):
    return _silu_of_half(0.5 * x)


def _softplus(x):
    return jnp.maximum(x, 0.0) + jnp.log1p(jnp.exp(-jnp.abs(x)))


def _conv3(x, cw, row_before=None, row_after=None):
    n = x.shape[0]
    sub = lax.broadcasted_iota(jnp.int32, (SUBLANES, 1), 0)
    zero = jnp.zeros((1, x.shape[1]), x.dtype)
    x_prev = pltpu.roll(x, 1, 0)
    first = jnp.where(sub == 0, zero if row_before is None else row_before, x_prev[:SUBLANES, :])
    x_prev = jnp.concatenate([first, x_prev[SUBLANES:, :]], axis=0)
    x_next = pltpu.roll(x, n - 1, 0)
    last = jnp.where(sub == SUBLANES - 1, zero if row_after is None else row_after, x_next[n - SUBLANES:, :])
    x_next = jnp.concatenate([x_next[:n - SUBLANES, :], last], axis=0)
    return x_prev * cw[0:1, :] + x * cw[1:2, :] + x_next * cw[2:3, :]


def _rms_scale(x):
    return x * lax.rsqrt(jnp.mean(x * x, axis=-1, keepdims=True) + NORM_EPS)


def _repack_kernel(w_ref, o_ref):
    o_ref[...] = w_ref[...].astype(o_ref.dtype)


def _repack_w_in(w_t, group_stride, groups, width, tr=512):
    d = w_t.shape[1]
    return pl.pallas_call(
        _repack_kernel,
        grid=(groups, width // tr),
        in_specs=[pl.BlockSpec(
            (pl.Element(tr), pl.Element(d)),
            lambda g, i: (pl.multiple_of(g * group_stride + i * tr, math.gcd(group_stride, tr)), 0))],
        out_specs=pl.BlockSpec((None, tr, d), lambda g, i: (g, i, 0)),
        out_shape=jax.ShapeDtypeStruct((groups, width, d), BF16),
        compiler_params=pltpu.CompilerParams(
            dimension_semantics=("parallel", "parallel"), vmem_limit_bytes=VMEM_LIMIT),
        name="repack_w_in",
    )(w_t)


def _gdn_gates(sm, alog_row, dtb_row):
    tm = sm.shape[0]
    col = lax.broadcasted_iota(jnp.int32, (1, LANES), 1)
    is_g = (col >= _SM_A_F) & (col < _SM_BETA_F)
    is_fwd = col < _SM_A_B
    is_beta = (col >= _SM_BETA_F) & (col < _SM_BETA_B + HEADS)
    g = jnp.where(is_g, -jnp.exp(alog_row) * _softplus(sm + dtb_row), 0.0)
    r = lax.broadcasted_iota(jnp.int32, (CUM_ROWS, CUM_ROWS), 0)
    c = lax.broadcasted_iota(jnp.int32, (CUM_ROWS, CUM_ROWS), 1)
    same_chunk = (r // CHUNK) == (c // CHUNK)
    tril_bf = (same_chunk & (r >= c)).astype(F32).astype(BF16)
    triu_bf = (same_chunk & (r <= c)).astype(F32).astype(BF16)
    slabs = []
    for s in range(tm // CUM_ROWS):
        hi, lo = _split(g[s * CUM_ROWS:(s + 1) * CUM_ROWS])
        prefix = _dot(tril_bf, hi) + _dot(tril_bf, lo)
        suffix = _dot(triu_bf, hi) + _dot(triu_bf, lo)
        slabs.append(jnp.where(is_fwd, prefix, suffix))
    g_cum = jnp.concatenate(slabs, axis=0)
    return jnp.where(is_g, g_cum, jnp.where(is_beta, _sigmoid(sm), sm))


def _in_proj_kernel(x_ref, g_ref, w_ref, ws_ref, alog_ref, dtb_ref, p_ref, sm_ref, hn_ref):
    @pl.when(pl.program_id(1) == 0)
    def _():
        hn = (_rms_scale(x_ref[...]) * g_ref[...]).astype(BF16)
        hn_ref[...] = hn
        ws = _bf(ws_ref[...])
        half = hn.shape[0] // 2
        sm = jnp.concatenate([_dot_nt(hn[:half], ws), _dot_nt(hn[half:], ws)], axis=0)
        sm_ref[...] = _gdn_gates(sm, alog_ref[...], dtb_ref[...])

    res = _dot_nt(hn_ref[...], w_ref[...])
    for cb in range(p_ref.shape[0]):
        p_ref[cb] = res[:, cb * LANES:(cb + 1) * LANES]


def _in_proj(x2, g, w_big, w_small, alog_row, dtb_row, tm=1024, tn=1024):
    m, d = x2.shape
    groups, n_group, _ = w_big.shape
    per_group = n_group // tn
    n = groups * n_group
    return pl.pallas_call(
        _in_proj_kernel,
        grid=(m // tm, n // tn),
        in_specs=[
            pl.BlockSpec((tm, d), lambda i, j: (i, 0)),
            pl.BlockSpec((1, d), lambda i, j: (0, 0)),
            pl.BlockSpec((None, tn, d), lambda i, j: (j // per_group, j % per_group, 0)),
            pl.BlockSpec((LANES, d), lambda i, j: (0, 0)),
            pl.BlockSpec((1, LANES), lambda i, j: (0, 0)),
            pl.BlockSpec((1, LANES), lambda i, j: (0, 0)),
        ],
        out_specs=[
            pl.BlockSpec((tn // LANES, tm, LANES), lambda i, j: (j, i, 0)),
            pl.BlockSpec((tm, LANES), lambda i, j: (i, 0)),
        ],
        out_shape=[
            jax.ShapeDtypeStruct((n // LANES, m, LANES), F32),
            jax.ShapeDtypeStruct((m, LANES), F32),
        ],
        scratch_shapes=[pltpu.VMEM((tm, d), BF16)],
        compiler_params=pltpu.CompilerParams(
            dimension_semantics=("parallel", "arbitrary"), vmem_limit_bytes=VMEM_LIMIT),
        name="in_proj",
    )(x2, g, w_big, w_small, alog_row, dtb_row)


def _chunk_masks():
    r = lax.broadcasted_iota(jnp.int32, (CHUNK, CHUNK), 0)
    c = lax.broadcasted_iota(jnp.int32, (CHUNK, CHUNK), 1)
    return r, c


def _gated_head_norm(o, z, g):
    return (_rms_scale(o) * g) * _silu(z)


def _gla_stages(chunks, states, q_ref, k_ref, v_ref, dirs, o_ref):
    pairs = []
    for u in range(len(chunks[0])):
        for d, (lg_ref, incl, tri_bf, edge) in enumerate(dirs):
            row0 = pl.multiple_of(chunks[d][u] * CHUNK, CHUNK)
            rows = pl.ds(row0, CHUNK)
            hi, lo = _split(lg_ref[rows, :])
            pairs.append(dict(d=d, row0=row0, rows=rows, incl=incl, edge=edge, hi=hi, lo=lo, tri=tri_bf))
    for p in pairs:
        p["g_cum"] = _dot(p["tri"], p["hi"]) + _dot(p["tri"], p["lo"])
    yield
    for p in pairs:
        g_cum = p["g_cum"]
        g_edge = g_cum[p["edge"]:p["edge"] + 1, :]
        k = k_ref[p["rows"], :]
        p["v_bf"] = _bf(v_ref[p["rows"], :])
        p["q_dec"] = _bf(q_ref[p["rows"], :] * (HEAD_DIM ** -0.5) * jnp.exp(g_cum))
        p["decay"] = jnp.exp(g_edge)
        p["scores"] = _dot_nt(p["q_dec"], _bf(k * jnp.exp(-g_cum)))
        p["upd"] = _dot_tn(p["v_bf"], _bf(k * jnp.exp(g_edge - g_cum)))
    yield
    for p in pairs:
        p["o"] = _dot(_bf(jnp.where(p["incl"], p["scores"], 0.0)), p["v_bf"])
    yield
    seq = q_ref.shape[0]
    for p in pairs:
        st = states[p["d"]]
        o_ref[pl.ds(p["d"] * seq + p["row0"], CHUNK), :] = p["o"] + _dot_nt(p["q_dec"], _bf(st))
        states[p["d"]] = st * p["decay"] + p["upd"]


GLA_STAGES = 4


def _unit_tri_inverses(l_mats, r, c, eye):
    blk16 = (r // 16) == (c // 16)
    blk32 = (r // 32) == (c // 32)
    a = [-jnp.where(blk16, l, 0.0) for l in l_mats]
    a_bf = [_bf(x) for x in a]
    p2_bf = [_bf(_dot(x, x)) for x in a_bf]
    yield
    t = [eye + x for x in a]
    t = [x + _dot(_bf(x), p) for x, p in zip(t, p2_bf)]
    p4_bf = [_bf(_dot(p, p)) for p in p2_bf]
    yield
    t = [x + _dot(_bf(x), p) for x, p in zip(t, p4_bf)]
    p8_bf = [_bf(_dot(p, p)) for p in p4_bf]
    yield
    t = [x + _dot(_bf(x), p) for x, p in zip(t, p8_bf)]
    yield
    for off_diag in (blk32 & ~blk16, ~blk32):
        t_bf = [_bf(x) for x in t]
        e_t = [_bf(_dot(_bf(jnp.where(off_diag, l, 0.0)), x)) for l, x in zip(l_mats, t_bf)]
        yield
        t = [x - _dot(xb, e) for x, xb, e in zip(t, t_bf, e_t)]
        yield
    return t


def _interleave(stages, actions):
    actions = list(actions)
    for _ in stages:
        if actions:
            actions.pop(0)()
    for act in actions:
        act()


def _gdn_prepare(chunks, qs_ref, ks_ref, vs_ref, dirs, o_ref, nq_ref, b_ref, d_ref, r, c, eye, n_chunks):
    seq = n_chunks * CHUNK
    pairs = []
    for u in range(len(chunks[0])):
        for d, (g_ref, beta_ref, incl, strict, edge) in enumerate(dirs):
            n = chunks[d][u]
            row0 = pl.multiple_of(n * CHUNK, CHUNK)
            rows = pl.ds(row0, CHUNK)
            pairs.append(dict(n=n, d=d, row0=row0, q=qs_ref[rows, :], k=ks_ref[rows, :], v=vs_ref[rows, :],
                              g_cum=g_ref[rows, :], beta=beta_ref[rows, :],
                              incl=incl, strict=strict, edge=edge))
    for p in pairs:
        k_bf = _bf(p["k"])
        p["kk"] = _dot_nt(k_bf, k_bf)
        p["qk"] = _dot_nt(_bf(p["q"]), k_bf)
    yield
    for p in pairs:
        g_cum, beta = p["g_cum"], p["beta"]
        g_cum_t = g_cum.T
        diff = g_cum[:, :CHUNK] - g_cum_t[:CHUNK, :]
        decay = jnp.where(p["incl"], jnp.exp(jnp.where(p["incl"], diff, 0.0)), 0.0)
        p["l_mat"] = jnp.where(p["strict"], p["kk"], 0.0) * decay * beta[:, :CHUNK]
        p["e_g"] = jnp.exp(g_cum)
        p["g_edge"] = g_cum[p["edge"]:p["edge"] + 1, :]
        p["rhs"] = _bf(jnp.concatenate([p["k"] * (beta * p["e_g"]), p["v"] * beta], axis=1))
        p["attn"] = _bf(p["qk"] * decay)
        p["k_tail"] = _bf(p["k"] * jnp.exp(p["g_edge"] - g_cum))
    t_inv = yield from _unit_tri_inverses([p["l_mat"] for p in pairs], r, c, eye)
    wu_bf = [_bf(_dot(_bf(t), p["rhs"])) for t, p in zip(t_inv, pairs)]
    yield
    aw = [_dot(p["attn"], x) for p, x in zip(pairs, wu_bf)]
    kw = [_dot_tn(p["k_tail"], x) for p, x in zip(pairs, wu_bf)]
    for p, aw_p, kw_p in zip(pairs, aw, kw):
        slot = p["d"] * n_chunks + p["n"]
        nq_rows = pl.multiple_of(slot * NQ_ROWS, NQ_ROWS)
        nq_ref[pl.ds(nq_rows, HEAD_DIM), :] = _bf(-kw_p[:, :LANES])
        nq_ref[pl.ds(nq_rows + HEAD_DIM, CHUNK), :] = _bf(p["q"] * p["e_g"] - aw_p[:, :LANES])
        b_ref[pl.ds(pl.multiple_of(slot * HEAD_DIM, HEAD_DIM), HEAD_DIM), :] = kw_p[:, LANES:]
        d_ref[pl.ds(slot, 1), :] = jnp.exp(p["g_edge"])
        o_ref[pl.ds(p["d"] * seq + p["row0"], CHUNK), :] = aw_p[:, LANES:]


def _gdn_step_actions(chunks, states, o_ref, nq_ref, b_ref, d_ref, n_chunks):
    seq = n_chunks * CHUNK

    def step(u):
        for d in range(2):
            n = chunks[d][u]
            slot = d * n_chunks + n
            nq = nq_ref[pl.ds(pl.multiple_of(slot * NQ_ROWS, NQ_ROWS), NQ_ROWS), :]
            res = _dot(nq, _bf(states[d]))
            o_ref[pl.ds((2 + d) * seq + pl.multiple_of(n * CHUNK, CHUNK), CHUNK), :] = res[HEAD_DIM:, :]
            b_mat = b_ref[pl.ds(pl.multiple_of(slot * HEAD_DIM, HEAD_DIM), HEAD_DIM), :]
            states[d] = states[d] * d_ref[pl.ds(slot, 1), :] + (res[:HEAD_DIM, :] + b_mat)

    return [functools.partial(step, u) for u in range(len(chunks[0]))]


def _mixers_kernel(gq_ref, gk_ref, gv_ref, ggate_ref, q_ref, k_ref, v_ref, z_ref, sm_ref,
                   dwf_ref, dbf_ref, dwb_ref, dbb_ref, gla_ng_ref, cq_ref, ck_ref, cv_ref, ng_ref,
                   y_gla_ref, y_ref,
                   lgf_ref, lgb_ref, go_ref,
                   qs_ref, ks_ref, vs_ref, gf_ref, gb_ref, bf_ref, bb_ref, o_ref, nq_ref, b_ref, d_ref):
    h = pl.program_id(1)
    seq = q_ref.shape[0]
    n_chunks = seq // CHUNK

    n_groups = n_chunks // MIX_GROUP
    slab_rows = MIX_GROUP * CHUNK

    def group(t):
        fwd = [t * MIX_GROUP + u for u in range(MIX_GROUP)]
        return fwd, [n_chunks - 1 - n for n in fwd]

    def gla_load(t):
        for lg_ref, dw_ref, db_ref, row0 in ((lgf_ref, dwf_ref, dbf_ref, t * slab_rows),
                                             (lgb_ref, dwb_ref, dbb_ref, seq - (t + 1) * slab_rows)):
            rows = pl.ds(pl.multiple_of(row0, slab_rows), slab_rows)
            pre = _dot(_bf(sm_ref[rows, :]), dw_ref[...]) + db_ref[...]
            lg_ref[rows, :] = -_softplus(-pre) / GLA_GATE_NORMALIZER

    r, c = _chunk_masks()
    lower, upper = r >= c, r <= c
    gla_dirs = ((lgf_ref, lower, lower.astype(F32).astype(BF16), CHUNK - 1),
                (lgb_ref, upper, upper.astype(F32).astype(BF16), 0))

    def gla_actions(t, states):
        gen = _gla_stages(group(t), states, gq_ref, gk_ref, gv_ref, gla_dirs, go_ref)
        return [functools.partial(next, gen, None) for _ in range(GLA_STAGES)]

    sel_row = lax.broadcasted_iota(jnp.int32, (LANES, 4 * LANES), 0)
    sel_grp = lax.broadcasted_iota(jnp.int32, (LANES, 4 * LANES), 1) // LANES
    sel = (sel_row == _SM_A_F + HEADS * sel_grp + h).astype(F32).astype(BF16)

    def l2_norm(t, scale=1.0):
        return t * (lax.rsqrt(jnp.sum(t * t, axis=-1, keepdims=True) + NORM_EPS) * scale)

    def load_slab(s):
        r0, r1 = s * slab_rows, (s + 1) * slab_rows

        def conv_silu(x_ref, c_ref):
            before = x_ref[r0 - 1:r0, :] if s > 0 else None
            after = x_ref[r1:r1 + 1, :] if r1 < seq else None
            return _silu_of_half(_conv3(x_ref[r0:r1, :], 0.5 * c_ref[...], before, after))

        qs_ref[r0:r1, :] = l2_norm(conv_silu(q_ref, cq_ref), HEAD_DIM ** -0.5)
        ks_ref[r0:r1, :] = l2_norm(conv_silu(k_ref, ck_ref))
        vs_ref[r0:r1, :] = conv_silu(v_ref, cv_ref)
        sm_hi, sm_lo = _split(sm_ref[r0:r1, :])
        cols = _dot(sm_hi, sel) + _dot(sm_lo, sel)
        for i, ref in enumerate((gf_ref, gb_ref, bf_ref, bb_ref)):
            ref[r0:r1, :] = cols[:, i * LANES:(i + 1) * LANES]

    eye = (r == c).astype(F32)
    dirs = ((gf_ref, bf_ref, lower, r > c, CHUNK - 1),
            (gb_ref, bb_ref, upper, r < c, 0))

    def prepare(t):
        return _gdn_prepare(group(t), qs_ref, ks_ref, vs_ref, dirs, o_ref, nq_ref, b_ref, d_ref,
                            r, c, eye, n_chunks)

    def woven(t, gdn_states, gla_states):
        steps = _gdn_step_actions(group(t), gdn_states, o_ref, nq_ref, b_ref, d_ref, n_chunks)
        stages = gla_actions(t, gla_states)

        def both(step, stage):
            step()
            stage()

        return [functools.partial(both, s, g) for s, g in zip(steps, stages)] + steps[len(stages):]

    def body(t, carry):
        gdn_states, gla_states = list(carry[:2]), list(carry[2:])
        acts = woven(t - 1, gdn_states, gla_states)
        acts.insert(1, functools.partial(gla_load, t))
        _interleave(prepare(t), acts)
        return tuple(gdn_states + gla_states)

    zero = jnp.zeros((HEAD_DIM, HEAD_DIM), F32)
    load_slab(0)
    load_slab(n_groups - 1)
    _interleave(prepare(0), [functools.partial(load_slab, s) for s in range(1, n_groups - 1)]
                + [functools.partial(gla_load, 0)])
    carry = lax.fori_loop(1, n_groups, body, (zero,) * 4)
    gdn_states, gla_states = list(carry[:2]), list(carry[2:])
    for act in woven(n_groups - 1, gdn_states, gla_states):
        act()
    o = o_ref[0:seq, :] + o_ref[seq:2 * seq, :] + o_ref[2 * seq:3 * seq, :] + o_ref[3 * seq:4 * seq, :]
    y_ref[...] = _gated_head_norm(o, z_ref[...], ng_ref[...]).astype(y_ref.dtype)
    go = go_ref[0:seq, :] + go_ref[seq:2 * seq, :]
    y_gla_ref[...] = _gated_head_norm(go, ggate_ref[...], gla_ng_ref[...]).astype(y_gla_ref.dtype)


def _mixers(p, sm, dw_f, db_f, dw_b, db_b, gla_norm_g, conv_w, gdn_norm_g, batch, seq):
    def col(base):
        return pl.BlockSpec((None, seq, LANES), lambda b, h: (base + h, b, 0))

    def conv_col(base):
        return pl.BlockSpec((3, LANES), lambda b, h: (0, base + h))

    head_vec = pl.BlockSpec((1, LANES), lambda b, h: (0, h))
    head_mat = pl.BlockSpec((LANES, LANES), lambda b, h: (0, h))
    shared_vec = pl.BlockSpec((1, LANES), lambda b, h: (0, 0))
    head_out = pl.BlockSpec((seq, LANES), lambda b, h: (b, h))
    slots = 2 * (seq // CHUNK)
    seq_buf = pltpu.VMEM((seq, LANES), F32)
    return pl.pallas_call(
        _mixers_kernel,
        grid=(batch, HEADS),
        in_specs=[col(_CB_GLA_Q), col(_CB_GLA_K), col(_CB_GLA_V), col(_CB_GLA_GATE),
                  col(_CB_GDN_Q), col(_CB_GDN_K), col(_CB_GDN_V), col(_CB_GDN_Z),
                  pl.BlockSpec((seq, LANES), lambda b, h: (b, 0)),
                  head_mat, head_vec, head_mat, head_vec, shared_vec,
                  conv_col(0), conv_col(HEADS), conv_col(2 * HEADS), shared_vec],
        out_specs=[head_out, head_out],
        out_shape=[jax.ShapeDtypeStruct((batch * seq, HEAD_W), BF16)] * 2,
        scratch_shapes=[
            seq_buf, seq_buf,
            pltpu.VMEM((2 * seq, LANES), F32),
            seq_buf, seq_buf, seq_buf,
            seq_buf, seq_buf, seq_buf, seq_buf,
            pltpu.VMEM((4 * seq, LANES), F32),
            pltpu.VMEM((slots * NQ_ROWS, LANES), BF16),
            pltpu.VMEM((slots * HEAD_DIM, LANES), F32),
            pltpu.VMEM((slots, LANES), F32)],
        compiler_params=pltpu.CompilerParams(
            dimension_semantics=("parallel", "arbitrary"), vmem_limit_bytes=VMEM_LIMIT),
        name="mixers",
    )(p, p, p, p, p, p, p, p, sm, dw_f, db_f, dw_b, db_b, gla_norm_g, conv_w, conv_w, conv_w, gdn_norm_g)


def _merge_out_kernel(yg_ref, yd_ref, gg_ref, gd_ref, x_ref, wbg_ref, wbd_ref, wo_ref, n2_ref,
                      x1_ref, h2_ref):
    branch_g = _dot(yg_ref[...], wbg_ref[...])
    branch_d = _dot(yd_ref[...], wbd_ref[...])
    merged = jnp.concatenate(
        [_sigmoid(gg_ref[cb]) * branch_g[:, cb * LANES:(cb + 1) * LANES]
         + _sigmoid(gd_ref[cb]) * branch_d[:, cb * LANES:(cb + 1) * LANES]
         for cb in range(gg_ref.shape[0])], axis=1)
    x1 = x_ref[...] + _dot(merged.astype(BF16), wo_ref[...])
    x1_ref[...] = x1
    h2_ref[...] = (_rms_scale(x1) * n2_ref[...]).astype(h2_ref.dtype)


def _merge_out(y_gla, y_gdn, p, x2, w_bg, w_bd, w_o, n2, tm=256):
    m, d = x2.shape
    gate_cb = (8 * HEAD_W) // d

    def resident(shape):
        return pl.BlockSpec(shape, lambda i: (0, 0), pipeline_mode=pl.Buffered(1))

    return pl.pallas_call(
        _merge_out_kernel,
        grid=(m // tm,),
        in_specs=[
            pl.BlockSpec((tm, HEAD_W), lambda i: (i, 0)),
            pl.BlockSpec((tm, HEAD_W), lambda i: (i, 0)),
            pl.BlockSpec((d // LANES, tm, LANES), lambda i: (gate_cb, i, 0)),
            pl.BlockSpec((d // LANES, tm, LANES), lambda i: (gate_cb + 1, i, 0)),
            pl.BlockSpec((tm, d), lambda i: (i, 0)),
            resident((HEAD_W, d)), resident((HEAD_W, d)), resident((d, d)), resident((1, d)),
        ],
        out_specs=[pl.BlockSpec((tm, d), lambda i: (i, 0)), pl.BlockSpec((tm, d), lambda i: (i, 0))],
        out_shape=[jax.ShapeDtypeStruct((m, d), F32), jax.ShapeDtypeStruct((m, d), BF16)],
        compiler_params=pltpu.CompilerParams(
            dimension_semantics=("parallel",), vmem_limit_bytes=VMEM_LIMIT),
        name="merge_out",
    )(y_gla, y_gdn, p, p, x2, w_bg, w_bd, w_o, n2)


def _ffn_up_kernel(h_ref, wg_ref, wv_ref, cg_ref, cv_ref, bg_ref, bv_ref, act_ref, u_ref):
    seq = h_ref.shape[0]
    n_slabs = seq // FFN_ROWS
    lo, hi = SUBLANES, SUBLANES + FFN_ROWS
    params = ((_bf(wg_ref[...]), 0.5 * cg_ref[...], 0.5 * bg_ref[...]), (_bf(wv_ref[...]), cv_ref[...], bv_ref[...]))
    zero_row = jnp.zeros((1, act_ref.shape[1]), F32)

    def project(s):
        h = h_ref[s * FFN_ROWS:(s + 1) * FFN_ROWS, :]
        edges = []
        for i, (w_bf, _, _) in enumerate(params):
            u = _dot(h, w_bf)
            u_ref[s % 2, i, lo:hi, :] = u
            edges.append((u[:1, :], u[FFN_ROWS - 1:, :]))
        return edges

    def finish(s, before, after):
        out = []
        for i, (_, cw, b) in enumerate(params):
            u_ref[s % 2, i, lo - 1:lo, :] = zero_row if before is None else before[i][1]
            u_ref[s % 2, i, hi:hi + 1, :] = zero_row if after is None else after[i][0]
            out.append(u_ref[s % 2, i, lo - 1:hi - 1, :] * cw[0:1, :] + u_ref[s % 2, i, lo:hi, :] * cw[1:2, :]
                       + u_ref[s % 2, i, lo + 1:hi + 1, :] * cw[2:3, :] + b)
        act_ref[s * FFN_ROWS:(s + 1) * FFN_ROWS, :] = (_silu_of_half(out[0]) * out[1]).astype(act_ref.dtype)

    edges = [project(0)]
    for s in range(1, n_slabs):
        edges.append(project(s))
        finish(s - 1, edges[s - 2] if s >= 2 else None, edges[s])
    finish(n_slabs - 1, edges[-2] if n_slabs >= 2 else None, None)


def _ffn_up(h2, w_up, conv_w, conv_b, batch, seq, tn=256):
    d = h2.shape[1]
    nj = D_FF // tn
    return pl.pallas_call(
        _ffn_up_kernel,
        grid=(batch, nj),
        in_specs=[
            pl.BlockSpec((seq, d), lambda b, j: (b, 0)),
            pl.BlockSpec((d, tn), lambda b, j: (0, j)),
            pl.BlockSpec((d, tn), lambda b, j: (0, nj + j)),
            pl.BlockSpec((3, tn), lambda b, j: (0, j)),
            pl.BlockSpec((3, tn), lambda b, j: (0, nj + j)),
            pl.BlockSpec((1, tn), lambda b, j: (0, j)),
            pl.BlockSpec((1, tn), lambda b, j: (0, nj + j)),
        ],
        out_specs=pl.BlockSpec((seq, tn), lambda b, j: (b, j)),
        out_shape=jax.ShapeDtypeStruct((batch * seq, D_FF), BF16),
        scratch_shapes=[pltpu.VMEM((2, 2, FFN_ROWS + 2 * SUBLANES, tn), F32)],
        compiler_params=pltpu.CompilerParams(
            dimension_semantics=("parallel", "arbitrary"), vmem_limit_bytes=VMEM_LIMIT),
        name="ffn_up",
    )(h2, w_up, w_up, conv_w, conv_w, conv_b, conv_b)


def _ffn_down_kernel(act_ref, w_ref, x1_ref, g_ref, out_ref, *, final_norm):
    x2 = x1_ref[...] + _dot(act_ref[...], w_ref[...])
    out_ref[...] = _rms_scale(x2) * g_ref[...] if final_norm else x2


def _ffn_down(act, w_down, x1, g, final_norm, tm=256):
    m, d = x1.shape
    return pl.pallas_call(
        functools.partial(_ffn_down_kernel, final_norm=final_norm),
        grid=(m // tm,),
        in_specs=[
            pl.BlockSpec((tm, D_FF), lambda i: (i, 0)),
            pl.BlockSpec((D_FF, d), lambda i: (0, 0), pipeline_mode=pl.Buffered(1)),
            pl.BlockSpec((tm, d), lambda i: (i, 0)),
            pl.BlockSpec((1, d), lambda i: (0, 0)),
        ],
        out_specs=pl.BlockSpec((tm, d), lambda i: (i, 0)),
        out_shape=jax.ShapeDtypeStruct((m, d), F32),
        compiler_params=pltpu.CompilerParams(
            dimension_semantics=("parallel",), vmem_limit_bytes=VMEM_LIMIT),
        name="ffn_down",
    )(act, w_down, x1, g)


def _pad_rows(w, row0, rows):
    return jnp.zeros((rows, w.shape[1]), w.dtype).at[row0:row0 + w.shape[0]].set(w)


def kernel(x, norm1_g, w_in, gla_decay_w_f, gla_decay_b_f, gla_decay_w_b, gla_decay_b_b, gla_norm_g,
           gdn_conv_w, gdn_a_log_f, gdn_dt_bias_f, gdn_a_log_b, gdn_dt_bias_b, gdn_norm_g,
           w_branch_gla, w_branch_gdn, w_out, norm2_g, w_up, ffn_conv_w, ffn_conv_b, w_down,
           final_norm_g):
    batch, seq, d = x.shape
    depth = w_in.shape[0]
    x2 = x.reshape(batch * seq, d)
    for l in range(depth):
        wl_t = jnp.swapaxes(w_in, 1, 2)[l]
        w_big = _repack_w_in(wl_t, _OFF_GDN, 3, N_GROUP)
        w_small = jnp.concatenate(
            [wl_t[_OFF_LR:_OFF_GDN], wl_t[_OFF_AB:_OFF_GATES],
             jnp.zeros((LANES - 2 * GLA_LOW_RANK - 4 * HEADS, d), wl_t.dtype)], axis=0)
        pad = jnp.zeros((LANES - _SM_BETA_F,), F32)

        def gate_row(fwd, bwd):
            return jnp.concatenate([jnp.zeros((_SM_A_F,), F32), fwd, bwd, pad])[None, :]

        p, sm = _in_proj(x2, norm1_g[l][None, :], w_big, w_small,
                         gate_row(gdn_a_log_f[l], gdn_a_log_b[l]),
                         gate_row(gdn_dt_bias_f[l], gdn_dt_bias_b[l]))

        y_gla, y_gdn = _mixers(
            p, sm,
            _pad_rows(gla_decay_w_f[l], 0, LANES).astype(BF16), gla_decay_b_f[l][None, :],
            _pad_rows(gla_decay_w_b[l], GLA_LOW_RANK, LANES).astype(BF16), gla_decay_b_b[l][None, :],
            gla_norm_g[l][None, :], gdn_conv_w[l], gdn_norm_g[l][None, :], batch, seq)

        x2, h2 = _merge_out(y_gla, y_gdn, p, x2,
                            w_branch_gla[l].astype(BF16), w_branch_gdn[l].astype(BF16),
                            w_out[l].astype(BF16), norm2_g[l][None, :])
        act = _ffn_up(h2, w_up[l], ffn_conv_w[l], ffn_conv_b[l][None, :], batch, seq)
        x2 = _ffn_down(act, w_down[l].astype(BF16), x2, final_norm_g[None, :], l == depth - 1)
    return x2.reshape(batch, seq, d)
```

```python
import functools
import math

import jax
import jax.numpy as jnp
from jax import lax
from jax.experimental import pallas as pl
from jax.experimental.pallas import tpu as pltpu

F32 = jnp.float32
BF16 = jnp.bfloat16

D_MODEL = 2048
HEADS = 8
HEAD_DIM = 128
HEAD_W = HEADS * HEAD_DIM
GLA_LOW_RANK = 16
GLA_GATE_NORMALIZER = 16.0
D_FF = 5632
CHUNK = 64
NORM_EPS = 1e-6
LANES = 128
SUBLANES = 8
CUM_ROWS = 256
FFN_ROWS = 256
MIX_GROUP = 8
NQ_ROWS = HEAD_DIM + CHUNK

_OFF_LR = 4 * HEAD_W
_OFF_GDN = _OFF_LR + 2 * GLA_LOW_RANK
_OFF_AB = _OFF_GDN + 4 * HEAD_W
_OFF_GATES = _OFF_AB + 4 * HEADS
N_GROUP = 4 * HEAD_W
assert _OFF_LR == N_GROUP and _OFF_AB - _OFF_GDN == N_GROUP and 2 * D_MODEL == N_GROUP
assert _OFF_GATES == 2 * _OFF_GDN
_SM_A_F, _SM_A_B, _SM_BETA_F, _SM_BETA_B = 32, 40, 48, 56
_CB_GLA_Q, _CB_GLA_K, _CB_GLA_V, _CB_GLA_GATE = 0, 8, 16, 24
_CB_GDN_Q, _CB_GDN_K, _CB_GDN_V, _CB_GDN_Z = 32, 40, 48, 56

VMEM_LIMIT = 56 * 1024 * 1024


def _dot(a, b):
    return jnp.dot(a, b, preferred_element_type=F32)


def _dot_nt(a, b):
    return lax.dot_general(a, b, (((1,), (1,)), ((), ())), preferred_element_type=F32)


def _dot_tn(a, b):
    return lax.dot_general(a, b, (((0,), (0,)), ((), ())), preferred_element_type=F32)


def _bf(x):
    return x.astype(BF16)


def _split(x):
    hi = x.astype(BF16)
    lo = (x - hi.astype(F32)).astype(BF16)
    return hi, lo


def _sigmoid(x):
    return 0.5 * jnp.tanh(0.5 * x) + 0.5


def _silu_of_half(h):
    return h * jnp.tanh(h) + h


def _silu(x):
    return _silu_of_half(0.5 * x)


def _softplus(x):
    return jnp.maximum(x, 0.0) + jnp.log1p(jnp.exp(-jnp.abs(x)))


def _conv3(x, cw, row_before=None, row_after=None):
    n = x.shape[0]
    sub = lax.broadcasted_iota(jnp.int32, (SUBLANES, 1), 0)
    zero = jnp.zeros((1, x.shape[1]), x.dtype)
    x_prev = pltpu.roll(x, 1, 0)
    first = jnp.where(sub == 0, zero if row_before is None else row_before, x_prev[:SUBLANES, :])
    x_prev = jnp.concatenate([first, x_prev[SUBLANES:, :]], axis=0)
    x_next = pltpu.roll(x, n - 1, 0)
    last = jnp.where(sub == SUBLANES - 1, zero if row_after is None else row_after, x_next[n - SUBLANES:, :])
    x_next = jnp.concatenate([x_next[:n - SUBLANES, :], last], axis=0)
    return x_prev * cw[0:1, :] + x * cw[1:2, :] + x_next * cw[2:3, :]


def _rms_scale(x):
    return x * lax.rsqrt(jnp.mean(x * x, axis=-1, keepdims=True) + NORM_EPS)


def _repack_kernel(w_ref, o_ref):
    o_ref[...] = w_ref[...].astype(o_ref.dtype)


def _repack_w_in(w_t, group_stride, groups, width, tr=512):
    d = w_t.shape[1]
    return pl.pallas_call(
        _repack_kernel,
        grid=(groups, width // tr),
        in_specs=[pl.BlockSpec(
            (pl.Element(tr), pl.Element(d)),
            lambda g, i: (pl.multiple_of(g * group_stride + i * tr, math.gcd(group_stride, tr)), 0))],
        out_specs=pl.BlockSpec((None, tr, d), lambda g, i: (g, i, 0)),
        out_shape=jax.ShapeDtypeStruct((groups, width, d), BF16),
        compiler_params=pltpu.CompilerParams(
            dimension_semantics=("parallel", "parallel"), vmem_limit_bytes=VMEM_LIMIT),
        name="repack_w_in",
    )(w_t)


def _gdn_gates(sm, alog_row, dtb_row):
    tm = sm.shape[0]
    col = lax.broadcasted_iota(jnp.int32, (1, LANES), 1)
    is_g = (col >= _SM_A_F) & (col < _SM_BETA_F)
    is_fwd = col < _SM_A_B
    is_beta = (col >= _SM_BETA_F) & (col < _SM_BETA_B + HEADS)
    g = jnp.where(is_g, -jnp.exp(alog_row) * _softplus(sm + dtb_row), 0.0)
    r = lax.broadcasted_iota(jnp.int32, (CUM_ROWS, CUM_ROWS), 0)
    c = lax.broadcasted_iota(jnp.int32, (CUM_ROWS, CUM_ROWS), 1)
    same_chunk = (r // CHUNK) == (c // CHUNK)
    tril_bf = (same_chunk & (r >= c)).astype(F32).astype(BF16)
    triu_bf = (same_chunk & (r <= c)).astype(F32).astype(BF16)
    slabs = []
    for s in range(tm // CUM_ROWS):
        hi, lo = _split(g[s * CUM_ROWS:(s + 1) * CUM_ROWS])
        prefix = _dot(tril_bf, hi) + _dot(tril_bf, lo)
        suffix = _dot(triu_bf, hi) + _dot(triu_bf, lo)
        slabs.append(jnp.where(is_fwd, prefix, suffix))
    g_cum = jnp.concatenate(slabs, axis=0)
    return jnp.where(is_g, g_cum, jnp.where(is_beta, _sigmoid(sm), sm))


def _in_proj_kernel(x_ref, g_ref, w_ref, ws_ref, alog_ref, dtb_ref, p_ref, sm_ref, hn_ref):
    @pl.when(pl.program_id(1) == 0)
    def _():
        hn = (_rms_scale(x_ref[...]) * g_ref[...]).astype(BF16)
        hn_ref[...] = hn
        ws = _bf(ws_ref[...])
        half = hn.shape[0] // 2
        sm = jnp.concatenate([_dot_nt(hn[:half], ws), _dot_nt(hn[half:], ws)], axis=0)
        sm_ref[...] = _gdn_gates(sm, alog_ref[...], dtb_ref[...])

    res = _dot_nt(hn_ref[...], w_ref[...])
    for cb in range(p_ref.shape[0]):
        p_ref[cb] = res[:, cb * LANES:(cb + 1) * LANES]


def _in_proj(x2, g, w_big, w_small, alog_row, dtb_row, tm=1024, tn=1024):
    m, d = x2.shape
    groups, n_group, _ = w_big.shape
    per_group = n_group // tn
    n = groups * n_group
    return pl.pallas_call(
        _in_proj_kernel,
        grid=(m // tm, n // tn),
        in_specs=[
            pl.BlockSpec((tm, d), lambda i, j: (i, 0)),
            pl.BlockSpec((1, d), lambda i, j: (0, 0)),
            pl.BlockSpec((None, tn, d), lambda i, j: (j // per_group, j % per_group, 0)),
            pl.BlockSpec((LANES, d), lambda i, j: (0, 0)),
            pl.BlockSpec((1, LANES), lambda i, j: (0, 0)),
            pl.BlockSpec((1, LANES), lambda i, j: (0, 0)),
        ],
        out_specs=[
            pl.BlockSpec((tn // LANES, tm, LANES), lambda i, j: (j, i, 0)),
            pl.BlockSpec((tm, LANES), lambda i, j: (i, 0)),
        ],
        out_shape=[
            jax.ShapeDtypeStruct((n // LANES, m, LANES), F32),
            jax.ShapeDtypeStruct((m, LANES), F32),
        ],
        scratch_shapes=[pltpu.VMEM((tm, d), BF16)],
        compiler_params=pltpu.CompilerParams(
            dimension_semantics=("parallel", "arbitrary"), vmem_limit_bytes=VMEM_LIMIT),
        name="in_proj",
    )(x2, g, w_big, w_small, alog_row, dtb_row)


def _chunk_masks():
    r = lax.broadcasted_iota(jnp.int32, (CHUNK, CHUNK), 0)
    c = lax.broadcasted_iota(jnp.int32, (CHUNK, CHUNK), 1)
    return r, c


def _gated_head_norm(o, z, g):
    return (_rms_scale(o) * g) * _silu(z)


def _gla_stages(chunks, states, q_ref, k_ref, v_ref, dirs, o_ref):
    pairs = []
    for u in range(len(chunks[0])):
        for d, (lg_ref, incl, tri_bf, edge) in enumerate(dirs):
            row0 = pl.multiple_of(chunks[d][u] * CHUNK, CHUNK)
            rows = pl.ds(row0, CHUNK)
            hi, lo = _split(lg_ref[rows, :])
            pairs.append(dict(d=d, row0=row0, rows=rows, incl=incl, edge=edge, hi=hi, lo=lo, tri=tri_bf))
    for p in pairs:
        p["g_cum"] = _dot(p["tri"], p["hi"]) + _dot(p["tri"], p["lo"])
    yield
    for p in pairs:
        g_cum = p["g_cum"]
        g_edge = g_cum[p["edge"]:p["edge"] + 1, :]
        k = k_ref[p["rows"], :]
        p["v_bf"] = _bf(v_ref[p["rows"], :])
        p["q_dec"] = _bf(q_ref[p["rows"], :] * (HEAD_DIM ** -0.5) * jnp.exp(g_cum))
        p["decay"] = jnp.exp(g_edge)
        p["scores"] = _dot_nt(p["q_dec"], _bf(k * jnp.exp(-g_cum)))
        p["upd"] = _dot_tn(p["v_bf"], _bf(k * jnp.exp(g_edge - g_cum)))
    yield
    for p in pairs:
        p["o"] = _dot(_bf(jnp.where(p["incl"], p["scores"], 0.0)), p["v_bf"])
    yield
    seq = q_ref.shape[0]
    for p in pairs:
        st = states[p["d"]]
        o_ref[pl.ds(p["d"] * seq + p["row0"], CHUNK), :] = p["o"] + _dot_nt(p["q_dec"], _bf(st))
        states[p["d"]] = st * p["decay"] + p["upd"]


GLA_STAGES = 4


def _unit_tri_inverses(l_mats, r, c, eye):
    blk16 = (r // 16) == (c // 16)
    blk32 = (r // 32) == (c // 32)
    a = [-jnp.where(blk16, l, 0.0) for l in l_mats]
    a_bf = [_bf(x) for x in a]
    p2_bf = [_bf(_dot(x, x)) for x in a_bf]
    yield
    t = [eye + x for x in a]
    t = [x + _dot(_bf(x), p) for x, p in zip(t, p2_bf)]
    p4_bf = [_bf(_dot(p, p)) for p in p2_bf]
    yield
    t = [x + _dot(_bf(x), p) for x, p in zip(t, p4_bf)]
    p8_bf = [_bf(_dot(p, p)) for p in p4_bf]
    yield
    t = [x + _dot(_bf(x), p) for x, p in zip(t, p8_bf)]
    yield
    for off_diag in (blk32 & ~blk16, ~blk32):
        t_bf = [_bf(x) for x in t]
        e_t = [_bf(_dot(_bf(jnp.where(off_diag, l, 0.0)), x)) for l, x in zip(l_mats, t_bf)]
        yield
        t = [x - _dot(xb, e) for x, xb, e in zip(t, t_bf, e_t)]
        yield
    return t


def _interleave(stages, actions):
    actions = list(actions)
    for _ in stages:
        if actions:
            actions.pop(0)()
    for act in actions:
        act()


def _gdn_prepare(chunks, qs_ref, ks_ref, vs_ref, dirs, o_ref, nq_ref, b_ref, d_ref, r, c, eye, n_chunks):
    seq = n_chunks * CHUNK
    pairs = []
    for u in range(len(chunks[0])):
        for d, (g_ref, beta_ref, incl, strict, edge) in enumerate(dirs):
            n = chunks[d][u]
            row0 = pl.multiple_of(n * CHUNK, CHUNK)
            rows = pl.ds(row0, CHUNK)
            pairs.append(dict(n=n, d=d, row0=row0, q=qs_ref[rows, :], k=ks_ref[rows, :], v=vs_ref[rows, :],
                              g_cum=g_ref[rows, :], beta=beta_ref[rows, :],
                              incl=incl, strict=strict, edge=edge))
    for p in pairs:
        k_bf = _bf(p["k"])
        p["kk"] = _dot_nt(k_bf, k_bf)
        p["qk"] = _dot_nt(_bf(p["q"]), k_bf)
    yield
    for p in pairs:
        g_cum, beta = p["g_cum"], p["beta"]
        g_cum_t = g_cum.T
        diff = g_cum[:, :CHUNK] - g_cum_t[:CHUNK, :]
        decay = jnp.where(p["incl"], jnp.exp(jnp.where(p["incl"], diff, 0.0)), 0.0)
        p["l_mat"] = jnp.where(p["strict"], p["kk"], 0.0) * decay * beta[:, :CHUNK]
        p["e_g"] = jnp.exp(g_cum)
        p["g_edge"] = g_cum[p["edge"]:p["edge"] + 1, :]
        p["rhs"] = _bf(jnp.concatenate([p["k"] * (beta * p["e_g"]), p["v"] * beta], axis=1))
        p["attn"] = _bf(p["qk"] * decay)
        p["k_tail"] = _bf(p["k"] * jnp.exp(p["g_edge"] - g_cum))
    t_inv = yield from _unit_tri_inverses([p["l_mat"] for p in pairs], r, c, eye)
    wu_bf = [_bf(_dot(_bf(t), p["rhs"])) for t, p in zip(t_inv, pairs)]
    yield
    aw = [_dot(p["attn"], x) for p, x in zip(pairs, wu_bf)]
    kw = [_dot_tn(p["k_tail"], x) for p, x in zip(pairs, wu_bf)]
    for p, aw_p, kw_p in zip(pairs, aw, kw):
        slot = p["d"] * n_chunks + p["n"]
        nq_rows = pl.multiple_of(slot * NQ_ROWS, NQ_ROWS)
        nq_ref[pl.ds(nq_rows, HEAD_DIM), :] = _bf(-kw_p[:, :LANES])
        nq_ref[pl.ds(nq_rows + HEAD_DIM, CHUNK), :] = _bf(p["q"] * p["e_g"] - aw_p[:, :LANES])
        b_ref[pl.ds(pl.multiple_of(slot * HEAD_DIM, HEAD_DIM), HEAD_DIM), :] = kw_p[:, LANES:]
        d_ref[pl.ds(slot, 1), :] = jnp.exp(p["g_edge"])
        o_ref[pl.ds(p["d"] * seq + p["row0"], CHUNK), :] = aw_p[:, LANES:]


def _gdn_step_actions(chunks, states, o_ref, nq_ref, b_ref, d_ref, n_chunks):
    seq = n_chunks * CHUNK

    def step(u):
        for d in range(2):
            n = chunks[d][u]
            slot = d * n_chunks + n
            nq = nq_ref[pl.ds(pl.multiple_of(slot * NQ_ROWS, NQ_ROWS), NQ_ROWS), :]
            res = _dot(nq, _bf(states[d]))
            o_ref[pl.ds((2 + d) * seq + pl.multiple_of(n * CHUNK, CHUNK), CHUNK), :] = res[HEAD_DIM:, :]
            b_mat = b_ref[pl.ds(pl.multiple_of(slot * HEAD_DIM, HEAD_DIM), HEAD_DIM), :]
            states[d] = states[d] * d_ref[pl.ds(slot, 1), :] + (res[:HEAD_DIM, :] + b_mat)

    return [functools.partial(step, u) for u in range(len(chunks[0]))]


def _mixers_kernel(gq_ref, gk_ref, gv_ref, ggate_ref, q_ref, k_ref, v_ref, z_ref, sm_ref,
                   dwf_ref, dbf_ref, dwb_ref, dbb_ref, gla_ng_ref, cq_ref, ck_ref, cv_ref, ng_ref,
                   y_gla_ref, y_ref,
                   lgf_ref, lgb_ref, go_ref,
                   qs_ref, ks_ref, vs_ref, gf_ref, gb_ref, bf_ref, bb_ref, o_ref, nq_ref, b_ref, d_ref):
    h = pl.program_id(1)
    seq = q_ref.shape[0]
    n_chunks = seq // CHUNK

    n_groups = n_chunks // MIX_GROUP
    slab_rows = MIX_GROUP * CHUNK

    def group(t):
        fwd = [t * MIX_GROUP + u for u in range(MIX_GROUP)]
        return fwd, [n_chunks - 1 - n for n in fwd]

    def gla_load(t):
        for lg_ref, dw_ref, db_ref, row0 in ((lgf_ref, dwf_ref, dbf_ref, t * slab_rows),
                                             (lgb_ref, dwb_ref, dbb_ref, seq - (t + 1) * slab_rows)):
            rows = pl.ds(pl.multiple_of(row0, slab_rows), slab_rows)
            pre = _dot(_bf(sm_ref[rows, :]), dw_ref[...]) + db_ref[...]
            lg_ref[rows, :] = -_softplus(-pre) / GLA_GATE_NORMALIZER

    r, c = _chunk_masks()
    lower, upper = r >= c, r <= c
    gla_dirs = ((lgf_ref, lower, lower.astype(F32).astype(BF16), CHUNK - 1),
                (lgb_ref, upper, upper.astype(F32).astype(BF16), 0))

    def gla_actions(t, states):
        gen = _gla_stages(group(t), states, gq_ref, gk_ref, gv_ref, gla_dirs, go_ref)
        return [functools.partial(next, gen, None) for _ in range(GLA_STAGES)]

    sel_row = lax.broadcasted_iota(jnp.int32, (LANES, 4 * LANES), 0)
    sel_grp = lax.broadcasted_iota(jnp.int32, (LANES, 4 * LANES), 1) // LANES
    sel = (sel_row == _SM_A_F + HEADS * sel_grp + h).astype(F32).astype(BF16)

    def l2_norm(t, scale=1.0):
        return t * (lax.rsqrt(jnp.sum(t * t, axis=-1, keepdims=True) + NORM_EPS) * scale)

    def load_slab(s):
        r0, r1 = s * slab_rows, (s + 1) * slab_rows

        def conv_silu(x_ref, c_ref):
            before = x_ref[r0 - 1:r0, :] if s > 0 else None
            after = x_ref[r1:r1 + 1, :] if r1 < seq else None
            return _silu_of_half(_conv3(x_ref[r0:r1, :], 0.5 * c_ref[...], before, after))

        qs_ref[r0:r1, :] = l2_norm(conv_silu(q_ref, cq_ref), HEAD_DIM ** -0.5)
        ks_ref[r0:r1, :] = l2_norm(conv_silu(k_ref, ck_ref))
        vs_ref[r0:r1, :] = conv_silu(v_ref, cv_ref)
        sm_hi, sm_lo = _split(sm_ref[r0:r1, :])
        cols = _dot(sm_hi, sel) + _dot(sm_lo, sel)
        for i, ref in enumerate((gf_ref, gb_ref, bf_ref, bb_ref)):
            ref[r0:r1, :] = cols[:, i * LANES:(i + 1) * LANES]

    eye = (r == c).astype(F32)
    dirs = ((gf_ref, bf_ref, lower, r > c, CHUNK - 1),
            (gb_ref, bb_ref, upper, r < c, 0))

    def prepare(t):
        return _gdn_prepare(group(t), qs_ref, ks_ref, vs_ref, dirs, o_ref, nq_ref, b_ref, d_ref,
                            r, c, eye, n_chunks)

    def woven(t, gdn_states, gla_states):
        steps = _gdn_step_actions(group(t), gdn_states, o_ref, nq_ref, b_ref, d_ref, n_chunks)
        stages = gla_actions(t, gla_states)

        def both(step, stage):
            step()
            stage()

        return [functools.partial(both, s, g) for s, g in zip(steps, stages)] + steps[len(stages):]

    def body(t, carry):
        gdn_states, gla_states = list(carry[:2]), list(carry[2:])
        acts = woven(t - 1, gdn_states, gla_states)
        acts.insert(1, functools.partial(gla_load, t))
        _interleave(prepare(t), acts)
        return tuple(gdn_states + gla_states)

    zero = jnp.zeros((HEAD_DIM, HEAD_DIM), F32)
    load_slab(0)
    load_slab(n_groups - 1)
    _interleave(prepare(0), [functools.partial(load_slab, s) for s in range(1, n_groups - 1)]
                + [functools.partial(gla_load, 0)])
    carry = lax.fori_loop(1, n_groups, body, (zero,) * 4)
    gdn_states, gla_states = list(carry[:2]), list(carry[2:])
    for act in woven(n_groups - 1, gdn_states, gla_states):
        act()
    o = o_ref[0:seq, :] + o_ref[seq:2 * seq, :] + o_ref[2 * seq:3 * seq, :] + o_ref[3 * seq:4 * seq, :]
    y_ref[...] = _gated_head_norm(o, z_ref[...], ng_ref[...]).astype(y_ref.dtype)
    go = go_ref[0:seq, :] + go_ref[seq:2 * seq, :]
    y_gla_ref[...] = _gated_head_norm(go, ggate_ref[...], gla_ng_ref[...]).astype(y_gla_ref.dtype)


def _mixers(p, sm, dw_f, db_f, dw_b, db_b, gla_norm_g, conv_w, gdn_norm_g, batch, seq):
    def col(base):
        return pl.BlockSpec((None, seq, LANES), lambda b, h: (base + h, b, 0))

    def conv_col(base):
        return pl.BlockSpec((3, LANES), lambda b, h: (0, base + h))

    head_vec = pl.BlockSpec((1, LANES), lambda b, h: (0, h))
    head_mat = pl.BlockSpec((LANES, LANES), lambda b, h: (0, h))
    shared_vec = pl.BlockSpec((1, LANES), lambda b, h: (0, 0))
    head_out = pl.BlockSpec((seq, LANES), lambda b, h: (b, h))
    slots = 2 * (seq // CHUNK)
    seq_buf = pltpu.VMEM((seq, LANES), F32)
    return pl.pallas_call(
        _mixers_kernel,
        grid=(batch, HEADS),
        in_specs=[col(_CB_GLA_Q), col(_CB_GLA_K), col(_CB_GLA_V), col(_CB_GLA_GATE),
                  col(_CB_GDN_Q), col(_CB_GDN_K), col(_CB_GDN_V), col(_CB_GDN_Z),
                  pl.BlockSpec((seq, LANES), lambda b, h: (b, 0)),
                  head_mat, head_vec, head_mat, head_vec, shared_vec,
                  conv_col(0), conv_col(HEADS), conv_col(2 * HEADS), shared_vec],
        out_specs=[head_out, head_out],
        out_shape=[jax.ShapeDtypeStruct((batch * seq, HEAD_W), BF16)] * 2,
        scratch_shapes=[
            seq_buf, seq_buf,
            pltpu.VMEM((2 * seq, LANES), F32),
            seq_buf, seq_buf, seq_buf,
            seq_buf, seq_buf, seq_buf, seq_buf,
            pltpu.VMEM((4 * seq, LANES), F32),
            pltpu.VMEM((slots * NQ_ROWS, LANES), BF16),
            pltpu.VMEM((slots * HEAD_DIM, LANES), F32),
            pltpu.VMEM((slots, LANES), F32)],
        compiler_params=pltpu.CompilerParams(
            dimension_semantics=("parallel", "arbitrary"), vmem_limit_bytes=VMEM_LIMIT),
        name="mixers",
    )(p, p, p, p, p, p, p, p, sm, dw_f, db_f, dw_b, db_b, gla_norm_g, conv_w, conv_w, conv_w, gdn_norm_g)


def _merge_out_kernel(yg_ref, yd_ref, gg_ref, gd_ref, x_ref, wbg_ref, wbd_ref, wo_ref, n2_ref,
                      x1_ref, h2_ref):
    branch_g = _dot(yg_ref[...], wbg_ref[...])
    branch_d = _dot(yd_ref[...], wbd_ref[...])
    merged = jnp.concatenate(
        [_sigmoid(gg_ref[cb]) * branch_g[:, cb * LANES:(cb + 1) * LANES]
         + _sigmoid(gd_ref[cb]) * branch_d[:, cb * LANES:(cb + 1) * LANES]
         for cb in range(gg_ref.shape[0])], axis=1)
    x1 = x_ref[...] + _dot(merged.astype(BF16), wo_ref[...])
    x1_ref[...] = x1
    h2_ref[...] = (_rms_scale(x1) * n2_ref[...]).astype(h2_ref.dtype)


def _merge_out(y_gla, y_gdn, p, x2, w_bg, w_bd, w_o, n2, tm=256):
    m, d = x2.shape
    gate_cb = (8 * HEAD_W) // d

    def resident(shape):
        return pl.BlockSpec(shape, lambda i: (0, 0), pipeline_mode=pl.Buffered(1))

    return pl.pallas_call(
        _merge_out_kernel,
        grid=(m // tm,),
        in_specs=[
            pl.BlockSpec((tm, HEAD_W), lambda i: (i, 0)),
            pl.BlockSpec((tm, HEAD_W), lambda i: (i, 0)),
            pl.BlockSpec((d // LANES, tm, LANES), lambda i: (gate_cb, i, 0)),
            pl.BlockSpec((d // LANES, tm, LANES), lambda i: (gate_cb + 1, i, 0)),
            pl.BlockSpec((tm, d), lambda i: (i, 0)),
            resident((HEAD_W, d)), resident((HEAD_W, d)), resident((d, d)), resident((1, d)),
        ],
        out_specs=[pl.BlockSpec((tm, d), lambda i: (i, 0)), pl.BlockSpec((tm, d), lambda i: (i, 0))],
        out_shape=[jax.ShapeDtypeStruct((m, d), F32), jax.ShapeDtypeStruct((m, d), BF16)],
        compiler_params=pltpu.CompilerParams(
            dimension_semantics=("parallel",), vmem_limit_bytes=VMEM_LIMIT),
        name="merge_out",
    )(y_gla, y_gdn, p, p, x2, w_bg, w_bd, w_o, n2)


def _ffn_up_kernel(h_ref, wg_ref, wv_ref, cg_ref, cv_ref, bg_ref, bv_ref, act_ref, u_ref):
    seq = h_ref.shape[0]
    sizes = [FFN_ROWS] * (seq // FFN_ROWS)
    starts = [sum(sizes[:s]) for s in range(len(sizes))]
    n_slabs = len(sizes)
    lo = SUBLANES
    params = ((_bf(wg_ref[...]), 0.5 * cg_ref[...], 0.5 * bg_ref[...]), (_bf(wv_ref[...]), cv_ref[...], bv_ref[...]))
    zero_row = jnp.zeros((1, act_ref.shape[1]), F32)

    def project(s):
        h = h_ref[starts[s]:starts[s] + sizes[s], :]
        edges = []
        for i, (w_bf, _, _) in enumerate(params):
            u = _dot(h, w_bf)
            u_ref[s % 2, i, lo:lo + sizes[s], :] = u
            edges.append((u[:1, :], u[sizes[s] - 1:, :]))
        return edges

    def finish(s, before, after):
        hi = lo + sizes[s]
        out = []
        for i, (_, cw, b) in enumerate(params):
            u_ref[s % 2, i, lo - 1:lo, :] = zero_row if before is None else before[i][1]
            u_ref[s % 2, i, hi:hi + 1, :] = zero_row if after is None else after[i][0]
            out.append(u_ref[s % 2, i, lo - 1:hi - 1, :] * cw[0:1, :] + u_ref[s % 2, i, lo:hi, :] * cw[1:2, :]
                       + u_ref[s % 2, i, lo + 1:hi + 1, :] * cw[2:3, :] + b)
        act_ref[starts[s]:starts[s] + sizes[s], :] = (_silu_of_half(out[0]) * out[1]).astype(act_ref.dtype)

    edges = [project(0)]
    for s in range(1, n_slabs):
        edges.append(project(s))
        finish(s - 1, edges[s - 2] if s >= 2 else None, edges[s])
    finish(n_slabs - 1, edges[-2] if n_slabs >= 2 else None, None)


def _ffn_up(h2, w_up, conv_w, conv_b, batch, seq, tn=256):
    d = h2.shape[1]
    nj = D_FF // tn
    return pl.pallas_call(
        _ffn_up_kernel,
        grid=(batch, nj),
        in_specs=[
            pl.BlockSpec((seq, d), lambda b, j: (b, 0)),
            pl.BlockSpec((d, tn), lambda b, j: (0, j)),
            pl.BlockSpec((d, tn), lambda b, j: (0, nj + j)),
            pl.BlockSpec((3, tn), lambda b, j: (0, j)),
            pl.BlockSpec((3, tn), lambda b, j: (0, nj + j)),
            pl.BlockSpec((1, tn), lambda b, j: (0, j)),
            pl.BlockSpec((1, tn), lambda b, j: (0, nj + j)),
        ],
        out_specs=pl.BlockSpec((seq, tn), lambda b, j: (b, j)),
        out_shape=jax.ShapeDtypeStruct((batch * seq, D_FF), BF16),
        scratch_shapes=[pltpu.VMEM((2, 2, FFN_ROWS + 2 * SUBLANES, tn), F32)],
        compiler_params=pltpu.CompilerParams(
            dimension_semantics=("parallel", "arbitrary"), vmem_limit_bytes=VMEM_LIMIT),
        name="ffn_up",
    )(h2, w_up, w_up, conv_w, conv_w, conv_b, conv_b)


def _ffn_down_kernel(act_ref, w_ref, x1_ref, g_ref, out_ref, *, final_norm):
    x2 = x1_ref[...] + _dot(act_ref[...], w_ref[...])
    out_ref[...] = _rms_scale(x2) * g_ref[...] if final_norm else x2


def _ffn_down(act, w_down, x1, g, final_norm, tm=256):
    m, d = x1.shape
    return pl.pallas_call(
        functools.partial(_ffn_down_kernel, final_norm=final_norm),
        grid=(m // tm,),
        in_specs=[
            pl.BlockSpec((tm, D_FF), lambda i: (i, 0)),
            pl.BlockSpec((D_FF, d), lambda i: (0, 0), pipeline_mode=pl.Buffered(1)),
            pl.BlockSpec((tm, d), lambda i: (i, 0)),
            pl.BlockSpec((1, d), lambda i: (0, 0)),
        ],
        out_specs=pl.BlockSpec((tm, d), lambda i: (i, 0)),
        out_shape=jax.ShapeDtypeStruct((m, d), F32),
        compiler_params=pltpu.CompilerParams(
            dimension_semantics=("parallel",), vmem_limit_bytes=VMEM_LIMIT),
        name="ffn_down",
    )(act, w_down, x1, g)


def _pad_rows(w, row0, rows):
    return jnp.zeros((rows, w.shape[1]), w.dtype).at[row0:row0 + w.shape[0]].set(w)


def kernel(x, norm1_g, w_in, gla_decay_w_f, gla_decay_b_f, gla_decay_w_b, gla_decay_b_b, gla_norm_g,
           gdn_conv_w, gdn_a_log_f, gdn_dt_bias_f, gdn_a_log_b, gdn_dt_bias_b, gdn_norm_g,
           w_branch_gla, w_branch_gdn, w_out, norm2_g, w_up, ffn_conv_w, ffn_conv_b, w_down,
           final_norm_g):
    batch, seq, d = x.shape
    depth = w_in.shape[0]
    x2 = x.reshape(batch * seq, d)
    for l in range(depth):
        wl_t = jnp.swapaxes(w_in, 1, 2)[l]
        w_big = _repack_w_in(wl_t, _OFF_GDN, 3, N_GROUP)
        w_small = jnp.concatenate(
            [wl_t[_OFF_LR:_OFF_GDN], wl_t[_OFF_AB:_OFF_GATES],
             jnp.zeros((LANES - 2 * GLA_LOW_RANK - 4 * HEADS, d), wl_t.dtype)], axis=0)
        pad = jnp.zeros((LANES - _SM_BETA_F,), F32)

        def gate_row(fwd, bwd):
            return jnp.concatenate([jnp.zeros((_SM_A_F,), F32), fwd, bwd, pad])[None, :]

        p, sm = _in_proj(x2, norm1_g[l][None, :], w_big, w_small,
                         gate_row(gdn_a_log_f[l], gdn_a_log_b[l]),
                         gate_row(gdn_dt_bias_f[l], gdn_dt_bias_b[l]))

        y_gla, y_gdn = _mixers(
            p, sm,
            _pad_rows(gla_decay_w_f[l], 0, LANES).astype(BF16), gla_decay_b_f[l][None, :],
            _pad_rows(gla_decay_w_b[l], GLA_LOW_RANK, LANES).astype(BF16), gla_decay_b_b[l][None, :],
            gla_norm_g[l][None, :], gdn_conv_w[l], gdn_norm_g[l][None, :], batch, seq)

        x2, h2 = _merge_out(y_gla, y_gdn, p, x2,
                            w_branch_gla[l].astype(BF16), w_branch_gdn[l].astype(BF16),
                            w_out[l].astype(BF16), norm2_g[l][None, :])
        act = _ffn_up(h2, w_up[l], ffn_conv_w[l], ffn_conv_b[l][None, :], batch, seq)
        x2 = _ffn_down(act, w_down[l].astype(BF16), x2, final_norm_g[None, :], l == depth - 1)
    return x2.reshape(batch, seq, d)
```

```python
import functools
import math

import jax
import jax.numpy as jnp
from jax import lax
from jax.experimental import pallas as pl
from jax.experimental.pallas import tpu as pltpu

F32 = jnp.float32
BF16 = jnp.bfloat16

D_MODEL = 2048
HEADS = 8
HEAD_DIM = 128
HEAD_W = HEADS * HEAD_DIM
GLA_LOW_RANK = 16
GLA_GATE_NORMALIZER = 16.0
D_FF = 5632
CHUNK = 64
NORM_EPS = 1e-6
LANES = 128
SUBLANES = 8
CUM_ROWS = 256
FFN_ROWS = 256
MIX_GROUP = 8
NQ_ROWS = HEAD_DIM + CHUNK

_OFF_LR = 4 * HEAD_W
_OFF_GDN = _OFF_LR + 2 * GLA_LOW_RANK
_OFF_AB = _OFF_GDN + 4 * HEAD_W
_OFF_GATES = _OFF_AB + 4 * HEADS
N_GROUP = 4 * HEAD_W
assert _OFF_LR == N_GROUP and _OFF_AB - _OFF_GDN == N_GROUP and 2 * D_MODEL == N_GROUP
assert _OFF_GATES == 2 * _OFF_GDN
_SM_A_F, _SM_A_B, _SM_BETA_F, _SM_BETA_B = 32, 40, 48, 56
_CB_GLA_Q, _CB_GLA_K, _CB_GLA_V, _CB_GLA_GATE = 0, 8, 16, 24
_CB_GDN_Q, _CB_GDN_K, _CB_GDN_V, _CB_GDN_Z = 32, 40, 48, 56

VMEM_LIMIT = 56 * 1024 * 1024


def _dot(a, b):
    return jnp.dot(a, b, preferred_element_type=F32)


def _dot_nt(a, b):
    return lax.dot_general(a, b, (((1,), (1,)), ((), ())), preferred_element_type=F32)


def _dot_tn(a, b):
    return lax.dot_general(a, b, (((0,), (0,)), ((), ())), preferred_element_type=F32)


def _bf(x):
    return x.astype(BF16)


def _split(x):
    hi = x.astype(BF16)
    lo = (x - hi.astype(F32)).astype(BF16)
    return hi, lo


def _sigmoid(x):
    return 0.5 * jnp.tanh(0.5 * x) + 0.5


def _silu_of_half(h):
    return h * jnp.tanh(h) + h


def _silu(x):
    return _silu_of_half(0.5 * x)


def _softplus(x):
    return jnp.maximum(x, 0.0) + jnp.log1p(jnp.exp(-jnp.abs(x)))


def _conv3(x, cw, row_before=None, row_after=None):
    n = x.shape[0]
    sub = lax.broadcasted_iota(jnp.int32, (SUBLANES, 1), 0)
    zero = jnp.zeros((1, x.shape[1]), x.dtype)
    x_prev = pltpu.roll(x, 1, 0)
    first = jnp.where(sub == 0, zero if row_before is None else row_before, x_prev[:SUBLANES, :])
    x_prev = jnp.concatenate([first, x_prev[SUBLANES:, :]], axis=0)
    x_next = pltpu.roll(x, n - 1, 0)
    last = jnp.where(sub == SUBLANES - 1, zero if row_after is None else row_after, x_next[n - SUBLANES:, :])
    x_next = jnp.concatenate([x_next[:n - SUBLANES, :], last], axis=0)
    return x_prev * cw[0:1, :] + x * cw[1:2, :] + x_next * cw[2:3, :]


def _rms_scale(x):
    return x * lax.rsqrt(jnp.mean(x * x, axis=-1, keepdims=True) + NORM_EPS)


def _repack_kernel(w_ref, o_ref):
    o_ref[...] = w_ref[...].astype(o_ref.dtype)


def _repack_w_in(w_t, group_stride, groups, width, tr=512):
    d = w_t.shape[1]
    return pl.pallas_call(
        _repack_kernel,
        grid=(groups, width // tr),
        in_specs=[pl.BlockSpec(
            (pl.Element(tr), pl.Element(d)),
            lambda g, i: (pl.multiple_of(g * group_stride + i * tr, math.gcd(group_stride, tr)), 0))],
        out_specs=pl.BlockSpec((None, tr, d), lambda g, i: (g, i, 0)),
        out_shape=jax.ShapeDtypeStruct((groups, width, d), BF16),
        compiler_params=pltpu.CompilerParams(
            dimension_semantics=("parallel", "parallel"), vmem_limit_bytes=VMEM_LIMIT),
        name="repack_w_in",
    )(w_t)


def _gdn_gates(sm, alog_row, dtb_row):
    tm = sm.shape[0]
    col = lax.broadcasted_iota(jnp.int32, (1, LANES), 1)
    is_g = (col >= _SM_A_F) & (col < _SM_BETA_F)
    is_fwd = col < _SM_A_B
    is_beta = (col >= _SM_BETA_F) & (col < _SM_BETA_B + HEADS)
    g = jnp.where(is_g, -jnp.exp(alog_row) * _softplus(sm + dtb_row), 0.0)
    r = lax.broadcasted_iota(jnp.int32, (CUM_ROWS, CUM_ROWS), 0)
    c = lax.broadcasted_iota(jnp.int32, (CUM_ROWS, CUM_ROWS), 1)
    same_chunk = (r // CHUNK) == (c // CHUNK)
    tril_bf = (same_chunk & (r >= c)).astype(F32).astype(BF16)
    triu_bf = (same_chunk & (r <= c)).astype(F32).astype(BF16)
    slabs = []
    for s in range(tm // CUM_ROWS):
        hi, lo = _split(g[s * CUM_ROWS:(s + 1) * CUM_ROWS])
        prefix = _dot(tril_bf, hi) + _dot(tril_bf, lo)
        suffix = _dot(triu_bf, hi) + _dot(triu_bf, lo)
        slabs.append(jnp.where(is_fwd, prefix, suffix))
    g_cum = jnp.concatenate(slabs, axis=0)
    return jnp.where(is_g, g_cum, jnp.where(is_beta, _sigmoid(sm), sm))


def _in_proj_kernel(x_ref, g_ref, w_ref, ws_ref, alog_ref, dtb_ref, p_ref, sm_ref, hn_ref):
    @pl.when(pl.program_id(1) == 0)
    def _():
        hn = (_rms_scale(x_ref[...]) * g_ref[...]).astype(BF16)
        hn_ref[...] = hn
        ws = _bf(ws_ref[...])
        half = hn.shape[0] // 2
        sm = jnp.concatenate([_dot_nt(hn[:half], ws), _dot_nt(hn[half:], ws)], axis=0)
        sm_ref[...] = _gdn_gates(sm, alog_ref[...], dtb_ref[...])

    res = _dot_nt(hn_ref[...], w_ref[...])
    for cb in range(p_ref.shape[0]):
        p_ref[cb] = res[:, cb * LANES:(cb + 1) * LANES]


def _in_proj(x2, g, w_big, w_small, alog_row, dtb_row, tm=1024, tn=1024):
    m, d = x2.shape
    groups, n_group, _ = w_big.shape
    per_group = n_group // tn
    n = groups * n_group
    return pl.pallas_call(
        _in_proj_kernel,
        grid=(m // tm, n // tn),
        in_specs=[
            pl.BlockSpec((tm, d), lambda i, j: (i, 0)),
            pl.BlockSpec((1, d), lambda i, j: (0, 0)),
            pl.BlockSpec((None, tn, d), lambda i, j: (j // per_group, j % per_group, 0)),
            pl.BlockSpec((LANES, d), lambda i, j: (0, 0)),
            pl.BlockSpec((1, LANES), lambda i, j: (0, 0)),
            pl.BlockSpec((1, LANES), lambda i, j: (0, 0)),
        ],
        out_specs=[
            pl.BlockSpec((tn // LANES, tm, LANES), lambda i, j: (j, i, 0)),
            pl.BlockSpec((tm, LANES), lambda i, j: (i, 0)),
        ],
        out_shape=[
            jax.ShapeDtypeStruct((n // LANES, m, LANES), F32),
            jax.ShapeDtypeStruct((m, LANES), F32),
        ],
        scratch_shapes=[pltpu.VMEM((tm, d), BF16)],
        compiler_params=pltpu.CompilerParams(
            dimension_semantics=("parallel", "arbitrary"), vmem_limit_bytes=VMEM_LIMIT),
        name="in_proj",
    )(x2, g, w_big, w_small, alog_row, dtb_row)


def _chunk_masks():
    r = lax.broadcasted_iota(jnp.int32, (CHUNK, CHUNK), 0)
    c = lax.broadcasted_iota(jnp.int32, (CHUNK, CHUNK), 1)
    return r, c


def _gated_head_norm(o, z, g):
    return (_rms_scale(o) * g) * _silu(z)


def _gla_stages(chunks, states, q_ref, k_ref, v_ref, dirs, o_ref):
    pairs = []
    for u in range(len(chunks[0])):
        for d, (lg_ref, incl, tri_bf, edge) in enumerate(dirs):
            row0 = pl.multiple_of(chunks[d][u] * CHUNK, CHUNK)
            rows = pl.ds(row0, CHUNK)
            hi, lo = _split(lg_ref[rows, :])
            pairs.append(dict(d=d, row0=row0, rows=rows, incl=incl, edge=edge, hi=hi, lo=lo, tri=tri_bf))
    for p in pairs:
        p["g_cum"] = _dot(p["tri"], p["hi"]) + _dot(p["tri"], p["lo"])
    yield
    for p in pairs:
        g_cum = p["g_cum"]
        g_edge = g_cum[p["edge"]:p["edge"] + 1, :]
        k = k_ref[p["rows"], :]
        p["v_bf"] = _bf(v_ref[p["rows"], :])
        p["q_dec"] = _bf(q_ref[p["rows"], :] * (HEAD_DIM ** -0.5) * jnp.exp(g_cum))
        p["decay"] = jnp.exp(g_edge)
        p["scores"] = _dot_nt(p["q_dec"], _bf(k * jnp.exp(-g_cum)))
        p["upd"] = _dot_tn(p["v_bf"], _bf(k * jnp.exp(g_edge - g_cum)))
    yield
    for p in pairs:
        p["o"] = _dot(_bf(jnp.where(p["incl"], p["scores"], 0.0)), p["v_bf"])
    yield
    seq = q_ref.shape[0]
    for p in pairs:
        st = states[p["d"]]
        o_ref[pl.ds(p["d"] * seq + p["row0"], CHUNK), :] = p["o"] + _dot_nt(p["q_dec"], _bf(st))
        states[p["d"]] = st * p["decay"] + p["upd"]


GLA_STAGES = 4


def _unit_tri_inverses(l_mats, r, c, eye):
    blk16 = (r // 16) == (c // 16)
    blk32 = (r // 32) == (c // 32)
    a = [-jnp.where(blk16, l, 0.0) for l in l_mats]
    a_bf = [_bf(x) for x in a]
    p2_bf = [_bf(_dot(x, x)) for x in a_bf]
    yield
    t = [eye + x for x in a]
    t = [x + _dot(_bf(x), p) for x, p in zip(t, p2_bf)]
    p4_bf = [_bf(_dot(p, p)) for p in p2_bf]
    yield
    t = [x + _dot(_bf(x), p) for x, p in zip(t, p4_bf)]
    p8_bf = [_bf(_dot(p, p)) for p in p4_bf]
    yield
    t = [x + _dot(_bf(x), p) for x, p in zip(t, p8_bf)]
    yield
    for off_diag in (blk32 & ~blk16, ~blk32):
        t_bf = [_bf(x) for x in t]
        e_t = [_bf(_dot(_bf(jnp.where(off_diag, l, 0.0)), x)) for l, x in zip(l_mats, t_bf)]
        yield
        t = [x - _dot(xb, e) for x, xb, e in zip(t, t_bf, e_t)]
        yield
    return t


def _interleave(stages, actions):
    actions = list(actions)
    for _ in stages:
        if actions:
            actions.pop(0)()
    for act in actions:
        act()


def _gdn_prepare(chunks, qs_ref, ks_ref, vs_ref, dirs, o_ref, nq_ref, b_ref, d_ref, r, c, eye, n_chunks):
    seq = n_chunks * CHUNK
    pairs = []
    for u in range(len(chunks[0])):
        for d, (g_ref, beta_ref, incl, strict, edge) in enumerate(dirs):
            n = chunks[d][u]
            row0 = pl.multiple_of(n * CHUNK, CHUNK)
            rows = pl.ds(row0, CHUNK)
            pairs.append(dict(n=n, d=d, row0=row0, rows=rows, g_ref=g_ref, beta_ref=beta_ref,
                              incl=incl, strict=strict, edge=edge))
    for p in pairs:
        k_bf = _bf(ks_ref[p["rows"], :])
        p["kk"] = _dot_nt(k_bf, k_bf)
        p["qk"] = _dot_nt(_bf(qs_ref[p["rows"], :]), k_bf)
    yield
    for p in pairs:
        g_cum = p["g_ref"][p["rows"], :]
        g_cum_t = g_cum.T
        diff = g_cum[:, :CHUNK] - g_cum_t[:CHUNK, :]
        decay = jnp.where(p["incl"], jnp.exp(jnp.where(p["incl"], diff, 0.0)), 0.0)
        p["l_mat"] = jnp.where(p["strict"], p["kk"], 0.0) * decay * p["beta_ref"][p["rows"], :CHUNK]
        p["attn"] = _bf(p["qk"] * decay)
    t_inv = yield from _unit_tri_inverses([p["l_mat"] for p in pairs], r, c, eye)
    wu_bf = []
    for t, p in zip(t_inv, pairs):
        k, beta, g_cum = ks_ref[p["rows"], :], p["beta_ref"][p["rows"], :], p["g_ref"][p["rows"], :]
        rhs = jnp.concatenate([k * (beta * jnp.exp(g_cum)), vs_ref[p["rows"], :] * beta], axis=1)
        wu_bf.append(_bf(_dot(_bf(t), _bf(rhs))))
    yield
    aw = [_dot(p["attn"], x) for p, x in zip(pairs, wu_bf)]
    kw = []
    for p, x in zip(pairs, wu_bf):
        g_cum = p["g_ref"][p["rows"], :]
        p["g_edge"] = g_cum[p["edge"]:p["edge"] + 1, :]
        kw.append(_dot_tn(_bf(ks_ref[p["rows"], :] * jnp.exp(p["g_edge"] - g_cum)), x))
    for p, aw_p, kw_p in zip(pairs, aw, kw):
        slot = p["d"] * n_chunks + p["n"]
        nq_rows = pl.multiple_of(slot * NQ_ROWS, NQ_ROWS)
        q_dec = qs_ref[p["rows"], :] * jnp.exp(p["g_ref"][p["rows"], :])
        nq_ref[pl.ds(nq_rows, HEAD_DIM), :] = _bf(-kw_p[:, :LANES])
        nq_ref[pl.ds(nq_rows + HEAD_DIM, CHUNK), :] = _bf(q_dec - aw_p[:, :LANES])
        b_ref[pl.ds(pl.multiple_of(slot * HEAD_DIM, HEAD_DIM), HEAD_DIM), :] = kw_p[:, LANES:]
        d_ref[pl.ds(slot, 1), :] = jnp.exp(p["g_edge"])
        o_ref[pl.ds(p["d"] * seq + p["row0"], CHUNK), :] = aw_p[:, LANES:]


def _gdn_step_actions(chunks, states, o_ref, nq_ref, b_ref, d_ref, n_chunks):
    seq = n_chunks * CHUNK

    def step(u):
        for d in range(2):
            n = chunks[d][u]
            slot = d * n_chunks + n
            nq = nq_ref[pl.ds(pl.multiple_of(slot * NQ_ROWS, NQ_ROWS), NQ_ROWS), :]
            res = _dot(nq, _bf(states[d]))
            o_ref[pl.ds((2 + d) * seq + pl.multiple_of(n * CHUNK, CHUNK), CHUNK), :] = res[HEAD_DIM:, :]
            b_mat = b_ref[pl.ds(pl.multiple_of(slot * HEAD_DIM, HEAD_DIM), HEAD_DIM), :]
            states[d] = states[d] * d_ref[pl.ds(slot, 1), :] + (res[:HEAD_DIM, :] + b_mat)

    return [functools.partial(step, u) for u in range(len(chunks[0]))]


def _mixers_kernel(gq_ref, gk_ref, gv_ref, ggate_ref, q_ref, k_ref, v_ref, z_ref, sm_ref,
                   dwf_ref, dbf_ref, dwb_ref, dbb_ref, gla_ng_ref, cq_ref, ck_ref, cv_ref, ng_ref,
                   y_gla_ref, y_ref,
                   lgf_ref, lgb_ref, go_ref,
                   qs_ref, ks_ref, vs_ref, gf_ref, gb_ref, bf_ref, bb_ref, o_ref, nq_ref, b_ref, d_ref):
    h = pl.program_id(1)
    seq = q_ref.shape[0]
    n_chunks = seq // CHUNK

    n_groups = n_chunks // MIX_GROUP
    slab_rows = MIX_GROUP * CHUNK

    def group(t):
        fwd = [t * MIX_GROUP + u for u in range(MIX_GROUP)]
        return fwd, [n_chunks - 1 - n for n in fwd]

    def gla_load(t):
        for lg_ref, dw_ref, db_ref, row0 in ((lgf_ref, dwf_ref, dbf_ref, t * slab_rows),
                                             (lgb_ref, dwb_ref, dbb_ref, seq - (t + 1) * slab_rows)):
            rows = pl.ds(pl.multiple_of(row0, slab_rows), slab_rows)
            pre = _dot(_bf(sm_ref[rows, :]), dw_ref[...]) + db_ref[...]
            lg_ref[rows, :] = -_softplus(-pre) / GLA_GATE_NORMALIZER

    r, c = _chunk_masks()
    lower, upper = r >= c, r <= c
    gla_dirs = ((lgf_ref, lower, lower.astype(F32).astype(BF16), CHUNK - 1),
                (lgb_ref, upper, upper.astype(F32).astype(BF16), 0))

    def gla_actions(t, states):
        gen = _gla_stages(group(t), states, gq_ref, gk_ref, gv_ref, gla_dirs, go_ref)
        return [functools.partial(next, gen, None) for _ in range(GLA_STAGES)]

    sel_row = lax.broadcasted_iota(jnp.int32, (LANES, 4 * LANES), 0)
    sel_grp = lax.broadcasted_iota(jnp.int32, (LANES, 4 * LANES), 1) // LANES
    sel = (sel_row == _SM_A_F + HEADS * sel_grp + h).astype(F32).astype(BF16)

    def l2_norm(t, scale=1.0):
        return t * (lax.rsqrt(jnp.sum(t * t, axis=-1, keepdims=True) + NORM_EPS) * scale)

    def load_slab(s):
        r0, r1 = s * slab_rows, (s + 1) * slab_rows

        def conv_silu(x_ref, c_ref):
            before = x_ref[r0 - 1:r0, :] if s > 0 else None
            after = x_ref[r1:r1 + 1, :] if r1 < seq else None
            return _silu_of_half(_conv3(x_ref[r0:r1, :], 0.5 * c_ref[...], before, after))

        qs_ref[r0:r1, :] = l2_norm(conv_silu(q_ref, cq_ref), HEAD_DIM ** -0.5)
        ks_ref[r0:r1, :] = l2_norm(conv_silu(k_ref, ck_ref))
        vs_ref[r0:r1, :] = conv_silu(v_ref, cv_ref)
        sm_hi, sm_lo = _split(sm_ref[r0:r1, :])
        cols = _dot(sm_hi, sel) + _dot(sm_lo, sel)
        for i, ref in enumerate((gf_ref, gb_ref, bf_ref, bb_ref)):
            ref[r0:r1, :] = cols[:, i * LANES:(i + 1) * LANES]

    eye = (r == c).astype(F32)
    dirs = ((gf_ref, bf_ref, lower, r > c, CHUNK - 1),
            (gb_ref, bb_ref, upper, r < c, 0))

    def prepare(t):
        return _gdn_prepare(group(t), qs_ref, ks_ref, vs_ref, dirs, o_ref, nq_ref, b_ref, d_ref,
                            r, c, eye, n_chunks)

    def woven(t, gdn_states, gla_states):
        steps = _gdn_step_actions(group(t), gdn_states, o_ref, nq_ref, b_ref, d_ref, n_chunks)
        stages = gla_actions(t, gla_states)

        def both(step, stage):
            step()
            stage()

        return [functools.partial(both, s, g) for s, g in zip(steps, stages)] + steps[len(stages):]

    def body(t, carry):
        gdn_states, gla_states = list(carry[:2]), list(carry[2:])
        acts = woven(t - 1, gdn_states, gla_states)
        acts.insert(1, functools.partial(gla_load, t))
        _interleave(prepare(t), acts)
        return tuple(gdn_states + gla_states)

    zero = jnp.zeros((HEAD_DIM, HEAD_DIM), F32)
    load_slab(0)
    load_slab(n_groups - 1)
    _interleave(prepare(0), [functools.partial(load_slab, s) for s in range(1, n_groups - 1)]
                + [functools.partial(gla_load, 0)])
    carry = lax.fori_loop(1, n_groups, body, (zero,) * 4)
    gdn_states, gla_states = list(carry[:2]), list(carry[2:])
    for act in woven(n_groups - 1, gdn_states, gla_states):
        act()
    o = o_ref[0:seq, :] + o_ref[seq:2 * seq, :] + o_ref[2 * seq:3 * seq, :] + o_ref[3 * seq:4 * seq, :]
    y_ref[...] = _gated_head_norm(o, z_ref[...], ng_ref[...]).astype(y_ref.dtype)
    go = go_ref[0:seq, :] + go_ref[seq:2 * seq, :]
    y_gla_ref[...] = _gated_head_norm(go, ggate_ref[...], gla_ng_ref[...]).astype(y_gla_ref.dtype)


def _mixers(p, sm, dw_f, db_f, dw_b, db_b, gla_norm_g, conv_w, gdn_norm_g, batch, seq):
    def col(base):
        return pl.BlockSpec((None, seq, LANES), lambda b, h: (base + h, b, 0))

    def conv_col(base):
        return pl.BlockSpec((3, LANES), lambda b, h: (0, base + h))

    head_vec = pl.BlockSpec((1, LANES), lambda b, h: (0, h))
    head_mat = pl.BlockSpec((LANES, LANES), lambda b, h: (0, h))
    shared_vec = pl.BlockSpec((1, LANES), lambda b, h: (0, 0))
    head_out = pl.BlockSpec((seq, LANES), lambda b, h: (b, h))
    slots = 2 * (seq // CHUNK)
    seq_buf = pltpu.VMEM((seq, LANES), F32)
    return pl.pallas_call(
        _mixers_kernel,
        grid=(batch, HEADS),
        in_specs=[col(_CB_GLA_Q), col(_CB_GLA_K), col(_CB_GLA_V), col(_CB_GLA_GATE),
                  col(_CB_GDN_Q), col(_CB_GDN_K), col(_CB_GDN_V), col(_CB_GDN_Z),
                  pl.BlockSpec((seq, LANES), lambda b, h: (b, 0)),
                  head_mat, head_vec, head_mat, head_vec, shared_vec,
                  conv_col(0), conv_col(HEADS), conv_col(2 * HEADS), shared_vec],
        out_specs=[head_out, head_out],
        out_shape=[jax.ShapeDtypeStruct((batch * seq, HEAD_W), BF16)] * 2,
        scratch_shapes=[
            seq_buf, seq_buf,
            pltpu.VMEM((2 * seq, LANES), F32),
            seq_buf, seq_buf, seq_buf,
            seq_buf, seq_buf, seq_buf, seq_buf,
            pltpu.VMEM((4 * seq, LANES), F32),
            pltpu.VMEM((slots * NQ_ROWS, LANES), BF16),
            pltpu.VMEM((slots * HEAD_DIM, LANES), F32),
            pltpu.VMEM((slots, LANES), F32)],
        compiler_params=pltpu.CompilerParams(
            dimension_semantics=("parallel", "arbitrary"), vmem_limit_bytes=VMEM_LIMIT),
        name="mixers",
    )(p, p, p, p, p, p, p, p, sm, dw_f, db_f, dw_b, db_b, gla_norm_g, conv_w, conv_w, conv_w, gdn_norm_g)


def _merge_out_kernel(yg_ref, yd_ref, gg_ref, gd_ref, x_ref, wbg_ref, wbd_ref, wo_ref, n2_ref,
                      x1_ref, h2_ref):
    branch_g = _dot(yg_ref[...], wbg_ref[...])
    branch_d = _dot(yd_ref[...], wbd_ref[...])
    merged = jnp.concatenate(
        [_sigmoid(gg_ref[cb]) * branch_g[:, cb * LANES:(cb + 1) * LANES]
         + _sigmoid(gd_ref[cb]) * branch_d[:, cb * LANES:(cb + 1) * LANES]
         for cb in range(gg_ref.shape[0])], axis=1)
    x1 = x_ref[...] + _dot(merged.astype(BF16), wo_ref[...])
    x1_ref[...] = x1
    h2_ref[...] = (_rms_scale(x1) * n2_ref[...]).astype(h2_ref.dtype)


def _merge_out(y_gla, y_gdn, p, x2, w_bg, w_bd, w_o, n2, tm=256):
    m, d = x2.shape
    gate_cb = (8 * HEAD_W) // d

    def resident(shape):
        return pl.BlockSpec(shape, lambda i: (0, 0), pipeline_mode=pl.Buffered(1))

    return pl.pallas_call(
        _merge_out_kernel,
        grid=(m // tm,),
        in_specs=[
            pl.BlockSpec((tm, HEAD_W), lambda i: (i, 0)),
            pl.BlockSpec((tm, HEAD_W), lambda i: (i, 0)),
            pl.BlockSpec((d // LANES, tm, LANES), lambda i: (gate_cb, i, 0)),
            pl.BlockSpec((d // LANES, tm, LANES), lambda i: (gate_cb + 1, i, 0)),
            pl.BlockSpec((tm, d), lambda i: (i, 0)),
            resident((HEAD_W, d)), resident((HEAD_W, d)), resident((d, d)), resident((1, d)),
        ],
        out_specs=[pl.BlockSpec((tm, d), lambda i: (i, 0)), pl.BlockSpec((tm, d), lambda i: (i, 0))],
        out_shape=[jax.ShapeDtypeStruct((m, d), F32), jax.ShapeDtypeStruct((m, d), BF16)],
        compiler_params=pltpu.CompilerParams(
            dimension_semantics=("parallel",), vmem_limit_bytes=VMEM_LIMIT),
        name="merge_out",
    )(y_gla, y_gdn, p, p, x2, w_bg, w_bd, w_o, n2)


def _ffn_up_kernel(h_ref, wg_ref, wv_ref, cg_ref, cv_ref, bg_ref, bv_ref, act_ref, u_ref):
    seq = h_ref.shape[0]
    sizes = [FFN_ROWS] * (seq // FFN_ROWS)
    starts = [sum(sizes[:s]) for s in range(len(sizes))]
    n_slabs = len(sizes)
    lo = SUBLANES
    params = ((_bf(wg_ref[...]), 0.5 * cg_ref[...], 0.5 * bg_ref[...]), (_bf(wv_ref[...]), cv_ref[...], bv_ref[...]))
    zero_row = jnp.zeros((1, act_ref.shape[1]), F32)

    def project(s):
        h = h_ref[starts[s]:starts[s] + sizes[s], :]
        edges = []
        for i, (w_bf, _, _) in enumerate(params):
            u = _dot(h, w_bf)
            u_ref[s % 2, i, lo:lo + sizes[s], :] = u
            edges.append((u[:1, :], u[sizes[s] - 1:, :]))
        return edges

    def finish(s, before, after):
        hi = lo + sizes[s]
        out = []
        for i, (_, cw, b) in enumerate(params):
            u_ref[s % 2, i, lo - 1:lo, :] = zero_row if before is None else before[i][1]
            u_ref[s % 2, i, hi:hi + 1, :] = zero_row if after is None else after[i][0]
            out.append(u_ref[s % 2, i, lo - 1:hi - 1, :] * cw[0:1, :] + u_ref[s % 2, i, lo:hi, :] * cw[1:2, :]
                       + u_ref[s % 2, i, lo + 1:hi + 1, :] * cw[2:3, :] + b)
        act_ref[starts[s]:starts[s] + sizes[s], :] = (_silu_of_half(out[0]) * out[1]).astype(act_ref.dtype)

    edges = [project(0)]
    for s in range(1, n_slabs):
        edges.append(project(s))
        finish(s - 1, edges[s - 2] if s >= 2 else None, edges[s])
    finish(n_slabs - 1, edges[-2] if n_slabs >= 2 else None, None)


def _ffn_up(h2, w_up, conv_w, conv_b, batch, seq, tn=256):
    d = h2.shape[1]
    nj = D_FF // tn
    return pl.pallas_call(
        _ffn_up_kernel,
        grid=(batch, nj),
        in_specs=[
            pl.BlockSpec((seq, d), lambda b, j: (b, 0)),
            pl.BlockSpec((d, tn), lambda b, j: (0, j)),
            pl.BlockSpec((d, tn), lambda b, j: (0, nj + j)),
            pl.BlockSpec((3, tn), lambda b, j: (0, j)),
            pl.BlockSpec((3, tn), lambda b, j: (0, nj + j)),
            pl.BlockSpec((1, tn), lambda b, j: (0, j)),
            pl.BlockSpec((1, tn), lambda b, j: (0, nj + j)),
        ],
        out_specs=pl.BlockSpec((seq, tn), lambda b, j: (b, j)),
        out_shape=jax.ShapeDtypeStruct((batch * seq, D_FF), BF16),
        scratch_shapes=[pltpu.VMEM((2, 2, FFN_ROWS + 2 * SUBLANES, tn), F32)],
        compiler_params=pltpu.CompilerParams(
            dimension_semantics=("parallel", "arbitrary"), vmem_limit_bytes=VMEM_LIMIT),
        name="ffn_up",
    )(h2, w_up, w_up, conv_w, conv_w, conv_b, conv_b)


def _ffn_down_kernel(act_ref, w_ref, x1_ref, g_ref, out_ref, *, final_norm):
    x2 = x1_ref[...] + _dot(act_ref[...], w_ref[...])
    out_ref[...] = _rms_scale(x2) * g_ref[...] if final_norm else x2


def _ffn_down(act, w_down, x1, g, final_norm, tm=256):
    m, d = x1.shape
    return pl.pallas_call(
        functools.partial(_ffn_down_kernel, final_norm=final_norm),
        grid=(m // tm,),
        in_specs=[
            pl.BlockSpec((tm, D_FF), lambda i: (i, 0)),
            pl.BlockSpec((D_FF, d), lambda i: (0, 0), pipeline_mode=pl.Buffered(1)),
            pl.BlockSpec((tm, d), lambda i: (i, 0)),
            pl.BlockSpec((1, d), lambda i: (0, 0)),
        ],
        out_specs=pl.BlockSpec((tm, d), lambda i: (i, 0)),
        out_shape=jax.ShapeDtypeStruct((m, d), F32),
        compiler_params=pltpu.CompilerParams(
            dimension_semantics=("parallel",), vmem_limit_bytes=VMEM_LIMIT),
        name="ffn_down",
    )(act, w_down, x1, g)


def _pad_rows(w, row0, rows):
    return jnp.zeros((rows, w.shape[1]), w.dtype).at[row0:row0 + w.shape[0]].set(w)


def kernel(x, norm1_g, w_in, gla_decay_w_f, gla_decay_b_f, gla_decay_w_b, gla_decay_b_b, gla_norm_g,
           gdn_conv_w, gdn_a_log_f, gdn_dt_bias_f, gdn_a_log_b, gdn_dt_bias_b, gdn_norm_g,
           w_branch_gla, w_branch_gdn, w_out, norm2_g, w_up, ffn_conv_w, ffn_conv_b, w_down,
           final_norm_g):
    batch, seq, d = x.shape
    depth = w_in.shape[0]
    x2 = x.reshape(batch * seq, d)
    for l in range(depth):
        wl_t = jnp.swapaxes(w_in, 1, 2)[l]
        w_big = _repack_w_in(wl_t, _OFF_GDN, 3, N_GROUP)
        w_small = jnp.concatenate(
            [wl_t[_OFF_LR:_OFF_GDN], wl_t[_OFF_AB:_OFF_GATES],
             jnp.zeros((LANES - 2 * GLA_LOW_RANK - 4 * HEADS, d), wl_t.dtype)], axis=0)
        pad = jnp.zeros((LANES - _SM_BETA_F,), F32)

        def gate_row(fwd, bwd):
            return jnp.concatenate([jnp.zeros((_SM_A_F,), F32), fwd, bwd, pad])[None, :]

        p, sm = _in_proj(x2, norm1_g[l][None, :], w_big, w_small,
                         gate_row(gdn_a_log_f[l], gdn_a_log_b[l]),
                         gate_row(gdn_dt_bias_f[l], gdn_dt_bias_b[l]))

        y_gla, y_gdn = _mixers(
            p, sm,
            _pad_rows(gla_decay_w_f[l], 0, LANES).astype(BF16), gla_decay_b_f[l][None, :],
            _pad_rows(gla_decay_w_b[l], GLA_LOW_RANK, LANES).astype(BF16), gla_decay_b_b[l][None, :],
            gla_norm_g[l][None, :], gdn_conv_w[l], gdn_norm_g[l][None, :], batch, seq)

        x2, h2 = _merge_out(y_gla, y_gdn, p, x2,
                            w_branch_gla[l].astype(BF16), w_branch_gdn[l].astype(BF16),
                            w_out[l].astype(BF16), norm2_g[l][None, :])
        act = _ffn_up(h2, w_up[l], ffn_conv_w[l], ffn_conv_b[l][None, :], batch, seq)
        x2 = _ffn_down(act, w_down[l].astype(BF16), x2, final_norm_g[None, :], l == depth - 1)
    return x2.reshape(batch, seq, d)
```
